```python
import math
import jax, jax.numpy as jnp
from jax import lax
import numpy as np

D_MODEL = 1024
BATCH = 4
SEQ = 8192
DEPTH = 1

HEAD_DIM = 64
A_HEADS = 8
A_KV_HEADS = 2
A_REP = A_HEADS // A_KV_HEADS
B_HEADS = 4
B_VDIM = 2 * HEAD_DIM
A_WIDTH = A_HEADS * HEAD_DIM
B_WIDTH = B_HEADS * B_VDIM
MIX_WIDTH = A_WIDTH + B_WIDTH
A_Q = A_HEADS * HEAD_DIM
A_KV = A_KV_HEADS * HEAD_DIM
B_QK = B_HEADS * 2 * HEAD_DIM
IN_WIDTH = A_Q + 2 * A_KV + 2 * B_QK + B_WIDTH
Q_BLOCK = 128
ATTN_SCALE = HEAD_DIM ** -0.5
GRID_W = 64
ROPE_THETA = 10000.0
NUM_BUCKETS = 32
MAX_DISTANCE = 128
N_EXPERTS = 16
EC_CAPACITY_FACTOR = 2
D_EXPERT = 2 * D_MODEL
PLE_DIM = 256
LN_EPS = 1e-5
QK_EPS = 1e-6
DEEPNORM_ALPHA = (2 * DEPTH) ** 0.25
DEEPNORM_BETA = (8 * DEPTH) ** -0.25

kernel_name = "hybrid_gqa_diffattn_ec_moe_encoder"


def layer_norm(x, g, b):
    xf = x.astype(jnp.float32)
    mu = jnp.mean(xf, axis=-1, keepdims=True)
    xc = xf - mu
    var = jnp.mean(xc * xc, axis=-1, keepdims=True)
    return (xc * lax.rsqrt(var + LN_EPS) * g.astype(jnp.float32) + b.astype(jnp.float32)).astype(x.dtype)


def rms_norm(x, g, eps):
    xf = x.astype(jnp.float32)
    ms = jnp.mean(xf * xf, axis=-1, keepdims=True)
    return (xf * lax.rsqrt(ms + eps) * g.astype(jnp.float32)).astype(x.dtype)


def rope_1d(x, pos):
    d = x.shape[-1]
    inv = ROPE_THETA ** (-jnp.arange(0, d, 2, dtype=jnp.float32) / d)
    ang = pos.astype(jnp.float32)[:, None] * inv[None, :]
    cos = jnp.cos(ang)[:, None, :]
    sin = jnp.sin(ang)[:, None, :]
    xf = x.astype(jnp.float32)
    x1, x2 = xf[..., : d // 2], xf[..., d // 2:]
    return jnp.concatenate([x1 * cos - x2 * sin, x2 * cos + x1 * sin], axis=-1).astype(x.dtype)


def axial_rope(x, row, col):
    half = x.shape[-1] // 2
    return jnp.concatenate([rope_1d(x[..., :half], row), rope_1d(x[..., half:], col)], axis=-1)


def t5_bucket(rel):
    half = NUM_BUCKETS // 2
    max_exact = half // 2
    ret = (rel > 0).astype(jnp.int32) * half
    n = jnp.abs(rel)
    nf = jnp.maximum(n, 1).astype(jnp.float32)
    large = max_exact + (jnp.log(nf / max_exact) / math.log(MAX_DISTANCE / max_exact)
                         * (half - max_exact)).astype(jnp.int32)
    large = jnp.minimum(large, half - 1)
    return ret + jnp.where(n < max_exact, n, large)


def mixer_gqa_axial(q, k, v, q_norm, k_norm, row, col):
    b_, s_ = q.shape[0], q.shape[1]
    q = axial_rope(rms_norm(q, q_norm, QK_EPS), row, col)
    k = axial_rope(rms_norm(k, k_norm, QK_EPS), row, col)
    nb = s_ // Q_BLOCK
    qb = jnp.moveaxis(q.reshape(b_, nb, Q_BLOCK, A_KV_HEADS, A_REP, HEAD_DIM), 1, 0)

    def block(qblk):
        s = jnp.einsum('bqgrd,bkgd->bgrqk', qblk, k).astype(jnp.float32) * ATTN_SCALE
        w = jax.nn.softmax(s, axis=-1).astype(v.dtype)
        return jnp.einsum('bgrqk,bkgd->bqgrd', w, v)

    o = lax.map(block, qb)
    return jnp.moveaxis(o, 0, 1).reshape(b_, s_, A_WIDTH)


def mixer_diff_attn(q, k, v, lq1, lk1, lq2, lk2, subln, rel_bias, lam_init):
    b_, s_ = q.shape[0], q.shape[1]
    f32 = jnp.float32
    lam = (jnp.exp(jnp.sum(lq1.astype(f32) * lk1.astype(f32)))
           - jnp.exp(jnp.sum(lq2.astype(f32) * lk2.astype(f32))) + lam_init)
    pos = jnp.arange(s_, dtype=jnp.int32)
    nb = s_ // Q_BLOCK
    qb = jnp.moveaxis(q.reshape(b_, nb, Q_BLOCK, B_HEADS, 2, HEAD_DIM), 1, 0)
    pb = pos.reshape(nb, Q_BLOCK)

    def block(args):
        qblk, qpos = args
        bucket = t5_bucket(pos[None, :] - qpos[:, None])
        bias = jnp.moveaxis(rel_bias[bucket], -1, 0).astype(f32)
        s = jnp.einsum('bqhmd,bkhmd->bhmqk', qblk, k).astype(f32) * ATTN_SCALE
        w = jax.nn.softmax(s + bias[None, :, None], axis=-1)
        a = (w[:, :, 0] - lam * w[:, :, 1]).astype(v.dtype)
        return jnp.einsum('bhqk,bkhe->bqhe', a, v)

    o = lax.map(block, (qb, pb))
    o = jnp.moveaxis(o, 0, 1).reshape(b_, s_, B_HEADS, B_VDIM)
    o = rms_norm(o, subln, LN_EPS) * (1.0 - lam_init)
    return o.reshape(b_, s_, B_WIDTH)


def expert_choice_moe(h, w_router, w_gate, w_up, w_down):
    b_, s_, _ = h.shape
    cap = EC_CAPACITY_FACTOR * s_ // N_EXPERTS
    logits = jnp.einsum('bsd,de->bse', h, w_router).astype(jnp.float32)
    aff = jax.nn.softmax(logits, axis=-1)
    gates, idx = lax.top_k(jnp.swapaxes(aff, 1, 2), cap)
    bidx = jnp.arange(b_)[:, None, None]
    xe = h[bidx, idx]
    hg = jnp.einsum('becd,edf->becf', xe, w_gate)
    hu = jnp.einsum('becd,edf->becf', xe, w_up)
    ye = jnp.einsum('becf,efd->becd', jax.nn.silu(hg) * hu, w_down)
    ye = ye * gates[..., None].astype(ye.dtype)
    return jnp.zeros_like(h).at[bidx, idx].add(ye)


def setup_inputs(seed: int = 0) -> dict:
    key = jax.random.key(seed)
    ks = jax.random.split(key, 24)
    f32 = jnp.float32
    nrm = lambda k, shape, s: jax.random.normal(k, shape, f32) * s
    beta = DEEPNORM_BETA
    col_scale = jnp.concatenate([
        jnp.ones((A_Q + A_KV,), f32), jnp.full((A_KV,), beta, f32),
        jnp.ones((2 * B_QK,), f32), jnp.full((B_WIDTH,), beta, f32)])
    return {
        "x": nrm(ks[0], (BATCH, SEQ, D_MODEL), 1.0),
        "p": nrm(ks[1], (DEPTH, BATCH, SEQ, PLE_DIM), 1.0),
        "w_in": nrm(ks[2], (DEPTH, D_MODEL, IN_WIDTH), D_MODEL ** -0.5) * col_scale,
        "w_out": nrm(ks[3], (DEPTH, MIX_WIDTH, D_MODEL), MIX_WIDTH ** -0.5 * beta),
        "a_q_norm": 1.0 + nrm(ks[4], (DEPTH, HEAD_DIM), 0.05),
        "a_k_norm": 1.0 + nrm(ks[5], (DEPTH, HEAD_DIM), 0.05),
        "b_lambda_q1": nrm(ks[6], (DEPTH, HEAD_DIM), 0.1),
        "b_lambda_k1": nrm(ks[7], (DEPTH, HEAD_DIM), 0.1),
        "b_lambda_q2": nrm(ks[8], (DEPTH, HEAD_DIM), 0.1),
        "b_lambda_k2": nrm(ks[9], (DEPTH, HEAD_DIM), 0.1),
        "b_subln": 1.0 + nrm(ks[10], (DEPTH, B_VDIM), 0.05),
        "rel_bias": nrm(ks[11], (NUM_BUCKETS, B_HEADS), 0.5),
        "ln1_g": 1.0 + nrm(ks[12], (DEPTH, D_MODEL), 0.05),
        "ln1_b": nrm(ks[13], (DEPTH, D_MODEL), 0.01),
        "w_router": nrm(ks[14], (DEPTH, D_MODEL, N_EXPERTS), D_MODEL ** -0.5),
        "w_gate": nrm(ks[15], (DEPTH, N_EXPERTS, D_MODEL, D_EXPERT), D_MODEL ** -0.5),
        "w_up": nrm(ks[16], (DEPTH, N_EXPERTS, D_MODEL, D_EXPERT), D_MODEL ** -0.5 * beta),
        "w_down": nrm(ks[17], (DEPTH, N_EXPERTS, D_EXPERT, D_MODEL), D_EXPERT ** -0.5 * beta),
        "ln2_g": 1.0 + nrm(ks[18], (DEPTH, D_MODEL), 0.05),
        "ln2_b": nrm(ks[19], (DEPTH, D_MODEL), 0.01),
        "w_ple_gate": nrm(ks[20], (DEPTH, D_MODEL, D_MODEL), D_MODEL ** -0.5),
        "w_ple_proj": nrm(ks[21], (DEPTH, PLE_DIM, D_MODEL), PLE_DIM ** -0.5 * beta),
        "ln3_g": 1.0 + nrm(ks[22], (DEPTH, D_MODEL), 0.05),
        "ln3_b": nrm(ks[23], (DEPTH, D_MODEL), 0.01),
    }


def reference(x, p, w_in, w_out, a_q_norm, a_k_norm, b_lambda_q1, b_lambda_k1,
              b_lambda_q2, b_lambda_k2, b_subln, rel_bias, ln1_g, ln1_b, w_router,
              w_gate, w_up, w_down, ln2_g, ln2_b, w_ple_gate, w_ple_proj, ln3_g, ln3_b):
    b_, s_, _ = x.shape
    rows = s_ // GRID_W
    row = jnp.repeat(jnp.arange(rows, dtype=jnp.int32), GRID_W)
    col = jnp.tile(jnp.arange(GRID_W, dtype=jnp.int32), rows)
    splits = [A_Q, A_Q + A_KV, A_Q + 2 * A_KV, A_Q + 2 * A_KV + B_QK, A_Q + 2 * A_KV + 2 * B_QK]
    for i in range(DEPTH):
        lam_init = 0.8 - 0.6 * math.exp(-0.3 * i)
        cols = jnp.einsum('bsd,dn->bsn', x, w_in[i])
        qa, ka, va, qb, kb, vb = jnp.split(cols, splits, axis=-1)
        oa = mixer_gqa_axial(qa.reshape(b_, s_, A_HEADS, HEAD_DIM),
                             ka.reshape(b_, s_, A_KV_HEADS, HEAD_DIM),
                             va.reshape(b_, s_, A_KV_HEADS, HEAD_DIM),
                             a_q_norm[i], a_k_norm[i], row, col)
        ob = mixer_diff_attn(qb.reshape(b_, s_, B_HEADS, 2, HEAD_DIM),
                             kb.reshape(b_, s_, B_HEADS, 2, HEAD_DIM),
                             vb.reshape(b_, s_, B_HEADS, B_VDIM),
                             b_lambda_q1[i], b_lambda_k1[i], b_lambda_q2[i], b_lambda_k2[i],
                             b_subln[i], rel_bias, lam_init)
        mix = jnp.einsum('bsm,md->bsd', jnp.concatenate([oa, ob], axis=-1), w_out[i])
        x = layer_norm(DEEPNORM_ALPHA * x + mix, ln1_g[i], ln1_b[i])
        moe = expert_choice_moe(x, w_router[i], w_gate[i], w_up[i], w_down[i])
        x = layer_norm(DEEPNORM_ALPHA * x + moe, ln2_g[i], ln2_b[i])
        gate = jax.nn.sigmoid(jnp.einsum('bsd,de->bse', x, w_ple_gate[i]))
        ple = jnp.einsum('bsk,kd->bsd', p[i], w_ple_proj[i]) * gate
        x = layer_norm(DEEPNORM_ALPHA * x + ple, ln3_g[i], ln3_b[i])
    return x
```

```python
import functools
import math

import jax
import jax.numpy as jnp
from jax import lax
from jax.experimental import pallas as pl
from jax.experimental.pallas import tpu as pltpu

F32 = jnp.float32
BF16 = jnp.bfloat16
I32 = jnp.int32

HEAD_DIM = 64
A_HEADS = 8
A_KV_HEADS = 2
A_REP = A_HEADS // A_KV_HEADS
B_HEADS = 4
B_VDIM = 2 * HEAD_DIM
A_Q = A_HEADS * HEAD_DIM
A_KV = A_KV_HEADS * HEAD_DIM
B_QK = B_HEADS * 2 * HEAD_DIM
B_WIDTH = B_HEADS * B_VDIM
ATTN_SCALE = HEAD_DIM ** -0.5
GRID_W = 64
ROPE_THETA = 10000.0
NUM_BUCKETS = 32
MAX_DISTANCE = 128
N_EXPERTS = 16
EC_CAPACITY_FACTOR = 2
LN_EPS = 1e-5
QK_EPS = 1e-6
LOG2E = math.log2(math.e)

LANES = 128
BF16_ROWS = 16
VMEM_LIMIT = 56 * 1024 * 1024

TM_PROJ = 512
TQ_A = 256
TK_A = 512
T_B = 512
T_TOK = 256
SLAB = 64
E_GROUP = 4
FC = 512
NEG_BIG = -1e30


def _cparams(sem):
    return pltpu.CompilerParams(dimension_semantics=sem, vmem_limit_bytes=VMEM_LIMIT)


def _dot(a, b):
    return jnp.dot(a, b, preferred_element_type=F32)


def _dot_nt(a, b):
    return lax.dot_general(a, b, (((1,), (1,)), ((), ())), preferred_element_type=F32)


def _layer_norm(y, g, b):
    mu = jnp.mean(y, axis=-1, keepdims=True)
    yc = y - mu
    var = jnp.mean(yc * yc, axis=-1, keepdims=True)
    return yc * lax.rsqrt(var + LN_EPS) * g + b


def _split_bf16(v):
    hi = v.astype(BF16)
    lo = (v - hi.astype(F32)).astype(BF16)
    return hi, lo


def _inproj_kernel(x_ref, w_ref, cs_ref, sn_ref, gq_ref, gk_ref, gmq_ref, gmk_ref,
                   qa_ref, kva_ref, qb_ref, kb_ref, vb_ref):
    tm = x_ref.shape[1]
    xb = x_ref[0].astype(BF16)
    lane = lax.broadcasted_iota(I32, (tm, LANES), 1)
    lo_half = lane < HEAD_DIM
    cs = cs_ref[...]
    sn = sn_ref[...]
    qscale = ATTN_SCALE * LOG2E

    def group_rms(v, gm_ref, gain):
        hi, lo = _split_bf16(v * v)
        ms = _dot(hi, gm_ref[...]) + _dot(lo, gm_ref[...])
        return v * lax.rsqrt(ms + QK_EPS) * gain

    def rope(v, reps):
        width = v.shape[1]
        lane_w = lax.broadcasted_iota(I32, v.shape, 1)
        first = (lane_w % (HEAD_DIM // 2)) < (HEAD_DIM // 4)
        rot = jnp.where(first, pltpu.roll(v, width - HEAD_DIM // 4, 1), pltpu.roll(v, HEAD_DIM // 4, 1))
        c = jnp.concatenate([cs] * reps, axis=1) if reps > 1 else cs
        s = jnp.concatenate([sn] * reps, axis=1) if reps > 1 else sn
        return v * c + rot * s

    qa = _dot(xb, w_ref[:, 0:A_Q])
    qa = rope(group_rms(qa, gmq_ref, gq_ref[...]), A_Q // LANES) * qscale
    for c in range(A_Q // LANES):
        chunk = qa[:, c * LANES:(c + 1) * LANES]
        qa_ref[0, 2 * c] = jnp.where(lo_half, chunk, 0.0).astype(BF16)
        qa_ref[0, 2 * c + 1] = jnp.where(lo_half, pltpu.roll(chunk, HEAD_DIM, 1), 0.0).astype(BF16)

    ka = _dot(xb, w_ref[:, A_Q:A_Q + A_KV])
    ka = rope(group_rms(ka, gmk_ref, gk_ref[...]), 1)
    va = _dot(xb, w_ref[:, A_Q + A_KV:A_Q + 2 * A_KV])
    kva_ref[0, 0] = jnp.where(lo_half, ka, pltpu.roll(va, HEAD_DIM, 1)).astype(BF16)
    kva_ref[0, 1] = jnp.where(lo_half, pltpu.roll(ka, HEAD_DIM, 1), va).astype(BF16)

    o = A_Q + 2 * A_KV
    qb = _dot(xb, w_ref[:, o:o + B_QK]) * qscale
    kb = _dot(xb, w_ref[:, o + B_QK:o + 2 * B_QK])
    vb = _dot(xb, w_ref[:, o + 2 * B_QK:o + 2 * B_QK + B_WIDTH])
    for h in range(B_HEADS):
        chunk = qb[:, h * LANES:(h + 1) * LANES]
        qb_ref[0, h, 0] = jnp.where(lo_half, chunk, 0.0).astype(BF16)
        qb_ref[0, h, 1] = jnp.where(lo_half, 0.0, chunk).astype(BF16)
        kb_ref[0, h] = kb[:, h * LANES:(h + 1) * LANES].astype(BF16)
        vb_ref[0, h] = vb[:, h * LANES:(h + 1) * LANES].astype(BF16)


def _inproj(x, w_in_bf, cs, sn, gq, gk, gmq, gmk):
    b_, s_, d_ = x.shape
    tm = min(TM_PROJ, s_)
    nst = s_ // tm
    in_w = w_in_bf.shape[1]
    full = lambda shape: pl.BlockSpec(shape, lambda b, i: (0,) * len(shape))
    return pl.pallas_call(
        _inproj_kernel,
        grid=(b_, nst),
        in_specs=[
            pl.BlockSpec((1, tm, d_), lambda b, i: (b, i, 0)),
            full((d_, in_w)),
            pl.BlockSpec((tm, LANES), lambda b, i: (i, 0)),
            pl.BlockSpec((tm, LANES), lambda b, i: (i, 0)),
            full((1, A_Q)), full((1, A_KV)), full((A_Q, A_Q)), full((A_KV, A_KV)),
        ],
        out_specs=[
            pl.BlockSpec((1, A_HEADS, tm, LANES), lambda b, i: (b, 0, i, 0)),
            pl.BlockSpec((1, A_KV_HEADS, tm, LANES), lambda b, i: (b, 0, i, 0)),
            pl.BlockSpec((1, B_HEADS, 2, tm, LANES), lambda b, i: (b, 0, 0, i, 0)),
            pl.BlockSpec((1, B_HEADS, tm, LANES), lambda b, i: (b, 0, i, 0)),
            pl.BlockSpec((1, B_HEADS, tm, LANES), lambda b, i: (b, 0, i, 0)),
        ],
        out_shape=[
            jax.ShapeDtypeStruct((b_, A_HEADS, s_, LANES), BF16),
            jax.ShapeDtypeStruct((b_, A_KV_HEADS, s_, LANES), BF16),
            jax.ShapeDtypeStruct((b_, B_HEADS, 2, s_, LANES), BF16),
            jax.ShapeDtypeStruct((b_, B_HEADS, s_, LANES), BF16),
            jax.ShapeDtypeStruct((b_, B_HEADS, s_, LANES), BF16),
        ],
        compiler_params=_cparams(("parallel", "parallel")),
        name="inproj",
    )(x, w_in_bf, cs, sn, gq, gk, gmq, gmk)


def _softmax_step(s, v_chunk, m_ref, l_ref, acc_ref):
    m_prev = m_ref[...]
    m_cur = jnp.maximum(m_prev, jnp.max(s, axis=1, keepdims=True))
    alpha = jnp.exp2(m_prev - m_cur)
    p = jnp.exp2(s - m_cur)
    l_ref[...] = alpha * l_ref[...] + jnp.sum(p, axis=1, keepdims=True)
    acc_ref[...] = alpha * acc_ref[...] + _dot(p.astype(BF16), v_chunk)
    m_ref[...] = m_cur


def _attn_a_kernel(q_ref, kv_ref, o_ref, m_ref, l_ref, acc_ref, *, tk):
    tq = q_ref.shape[2]
    rows = A_REP * tq
    nk = kv_ref.shape[2] // tk
    q = q_ref[0].reshape(rows, LANES)
    m_ref[...] = jnp.full(m_ref.shape, NEG_BIG, F32)
    l_ref[...] = jnp.zeros(l_ref.shape, F32)
    acc_ref[...] = jnp.zeros(acc_ref.shape, F32)

    def body(kj, carry):
        kv = kv_ref[0, 0, pl.ds(pl.multiple_of(kj * tk, tk), tk), :]
        _softmax_step(_dot_nt(q, kv), kv, m_ref, l_ref, acc_ref)
        return carry

    lax.fori_loop(0, nk, body, 0)
    o = acc_ref[...] / l_ref[...]
    lane = lax.broadcasted_iota(I32, (tq, LANES), 1)
    lo_half = lane < HEAD_DIM
    for c in range(A_REP // 2):
        even = o[(2 * c) * tq:(2 * c + 1) * tq]
        odd = o[(2 * c + 1) * tq:(2 * c + 2) * tq]
        o_ref[0, :, c * LANES:(c + 1) * LANES] = jnp.where(
            lo_half, pltpu.roll(even, HEAD_DIM, 1), odd).astype(BF16)


def _attn_a(qa, kva):
    b_, _, s_, _ = qa.shape
    tq = min(TQ_A, s_)
    tk = min(TK_A, s_)
    rows = A_REP * tq
    return pl.pallas_call(
        functools.partial(_attn_a_kernel, tk=tk),
        grid=(b_, A_KV_HEADS, s_ // tq),
        in_specs=[
            pl.BlockSpec((1, A_REP, tq, LANES), lambda b, g, i: (b, g, i, 0)),
            pl.BlockSpec((1, 1, s_, LANES), lambda b, g, i: (b, g, 0, 0)),
        ],
        out_specs=pl.BlockSpec((1, tq, A_REP * HEAD_DIM), lambda b, g, i: (b, i, g)),
        out_shape=jax.ShapeDtypeStruct((b_, s_, A_Q), BF16),
        scratch_shapes=[pltpu.VMEM((rows, 1), F32), pltpu.VMEM((rows, 1), F32),
                        pltpu.VMEM((rows, LANES), F32)],
        compiler_params=_cparams(("parallel", "parallel", "arbitrary")),
        name="attn_a",
    )(qa, kva)


def _attn_b_kernel(thr_ref, val_ref, q_ref, k_ref, v_ref, lq1_ref, lk1_ref, lq2_ref, lk2_ref, sub_ref,
                   o_ref, bias_ref, m_ref, l_ref, acc_ref, *, lam_init):
    h = pl.program_id(1)
    qi = pl.program_id(2)
    t = q_ref.shape[3]
    nk = k_ref.shape[2] // t
    half = NUM_BUCKETS // 2

    @pl.when(qi == 0)
    def _build_bias():
        row = lax.broadcasted_iota(I32, (t, t), 0)
        col = lax.broadcasted_iota(I32, (t, t), 1)
        for d in (-1, 0, 1):
            rel = col - row + d * t
            n = jnp.abs(rel)
            neg = jnp.full((t, t), val_ref[h, 0], F32)
            pos = jnp.full((t, t), val_ref[h, half], F32)
            for j in range(1, half):
                ge = n >= thr_ref[j]
                neg = jnp.where(ge, val_ref[h, j], neg)
                pos = jnp.where(ge, val_ref[h, half + j], pos)
            bias_ref[d + 1] = jnp.where(rel > 0, pos, neg)

    q = q_ref[0, 0].reshape(2 * t, LANES)
    m_ref[...] = jnp.full(m_ref.shape, NEG_BIG, F32)
    l_ref[...] = jnp.zeros(l_ref.shape, F32)
    acc_ref[...] = jnp.zeros(acc_ref.shape, F32)

    def chunk(kj):
        off = pl.multiple_of(kj * t, t)
        return k_ref[0, 0, pl.ds(off, t), :], v_ref[0, 0, pl.ds(off, t), :]

    def far_body(c, kj, carry):
        kc, vc = chunk(kj)
        _softmax_step(_dot_nt(q, kc) + c, vc, m_ref, l_ref, acc_ref)
        return carry

    def near_body(kj, carry):
        kc, vc = chunk(kj)
        bias = bias_ref[kj - qi + 1]
        s = _dot_nt(q, kc) + jnp.concatenate([bias, bias], axis=0)
        _softmax_step(s, vc, m_ref, l_ref, acc_ref)
        return carry

    near_lo = jnp.maximum(qi - 1, 0)
    near_hi = jnp.minimum(qi + 2, nk)
    lax.fori_loop(0, near_lo, functools.partial(far_body, val_ref[h, half - 1]), 0)
    lax.fori_loop(near_lo, near_hi, near_body, 0)
    lax.fori_loop(near_hi, nk, functools.partial(far_body, val_ref[h, NUM_BUCKETS - 1]), 0)

    lam = (jnp.exp(jnp.sum(lq1_ref[...] * lk1_ref[...], axis=1, keepdims=True))
           - jnp.exp(jnp.sum(lq2_ref[...] * lk2_ref[...], axis=1, keepdims=True)) + lam_init)
    o = acc_ref[...] / l_ref[...]
    o = o[0:t] - lam * o[t:2 * t]
    ms = jnp.mean(o * o, axis=1, keepdims=True)
    o_ref[0] = (o * lax.rsqrt(ms + LN_EPS) * sub_ref[...] * (1.0 - lam_init)).astype(BF16)


def _attn_b(qb, kb, vb, thr, vals, lq1, lk1, lq2, lk2, subln, lam_init):
    b_, _, _, s_, _ = qb.shape
    t = min(T_B, s_)
    vec = lambda n: pl.BlockSpec((1, n), lambda b, h, i, *_: (0, 0))
    grid_spec = pltpu.PrefetchScalarGridSpec(
        num_scalar_prefetch=2,
        grid=(b_, B_HEADS, s_ // t),
        in_specs=[
            pl.BlockSpec((1, 1, 2, t, LANES), lambda b, h, i, *_: (b, h, 0, i, 0)),
            pl.BlockSpec((1, 1, s_, LANES), lambda b, h, i, *_: (b, h, 0, 0)),
            pl.BlockSpec((1, 1, s_, LANES), lambda b, h, i, *_: (b, h, 0, 0)),
            vec(HEAD_DIM), vec(HEAD_DIM), vec(HEAD_DIM), vec(HEAD_DIM), vec(B_VDIM),
        ],
        out_specs=pl.BlockSpec((1, t, B_VDIM), lambda b, h, i, *_: (b, i, h)),
        scratch_shapes=[pltpu.VMEM((3, t, t), F32), pltpu.VMEM((2 * t, 1), F32),
                        pltpu.VMEM((2 * t, 1), F32), pltpu.VMEM((2 * t, LANES), F32)],
    )
    return pl.pallas_call(
        functools.partial(_attn_b_kernel, lam_init=lam_init),
        grid_spec=grid_spec,
        out_shape=jax.ShapeDtypeStruct((b_, s_, B_WIDTH), BF16),
        compiler_params=_cparams(("parallel", "parallel", "arbitrary")),
        name="attn_b",
    )(thr, vals, qb, kb, vb, lq1, lk1, lq2, lk2, subln)


def _outproj_kernel(oa_ref, ob_ref, x_ref, wa_ref, wb_ref, g_ref, b_ref, rh_ref, rl_ref,
                    x1_ref, x1b_ref, aff_ref, *, alpha):
    tm = x_ref.shape[1]
    mix = _dot(oa_ref[0], wa_ref[...]) + _dot(ob_ref[0], wb_ref[...])
    x1 = _layer_norm(alpha * x_ref[0] + mix, g_ref[...], b_ref[...])
    x1_ref[0] = x1
    x1b_ref[0] = x1.astype(BF16)
    hi, lo = _split_bf16(x1)
    logits = _dot(hi, rh_ref[...]) + _dot(lo, rh_ref[...]) + _dot(hi, rl_ref[...])
    lane = lax.broadcasted_iota(I32, (tm, LANES), 1)
    logits = jnp.where(lane < N_EXPERTS, logits, NEG_BIG)
    e = jnp.exp(logits - jnp.max(logits, axis=1, keepdims=True))
    aff = e / jnp.sum(e, axis=1, keepdims=True)
    aff_t = aff.T
    for c in range(tm // LANES):
        aff_ref[0, c] = aff_t[0:N_EXPERTS, c * LANES:(c + 1) * LANES]


def _outproj(oa, ob, x, wo_a, wo_b, g, b, rh, rl, alpha):
    b_, s_, d_ = x.shape
    tm = min(TM_PROJ, s_)
    nch = tm // LANES
    full = lambda shape: pl.BlockSpec(shape, lambda b, i: (0,) * len(shape))
    return pl.pallas_call(
        functools.partial(_outproj_kernel, alpha=alpha),
        grid=(b_, s_ // tm),
        in_specs=[
            pl.BlockSpec((1, tm, A_Q), lambda b, i: (b, i, 0)),
            pl.BlockSpec((1, tm, B_WIDTH), lambda b, i: (b, i, 0)),
            pl.BlockSpec((1, tm, d_), lambda b, i: (b, i, 0)),
            full((A_Q, d_)), full((B_WIDTH, d_)), full((1, d_)), full((1, d_)),
            full((d_, LANES)), full((d_, LANES)),
        ],
        out_specs=[
            pl.BlockSpec((1, tm, d_), lambda b, i: (b, i, 0)),
            pl.BlockSpec((1, tm, d_), lambda b, i: (b, i, 0)),
            pl.BlockSpec((1, nch, N_EXPERTS, LANES), lambda b, i: (b, i, 0, 0)),
        ],
        out_shape=[
            jax.ShapeDtypeStruct((b_, s_, d_), F32),
            jax.ShapeDtypeStruct((b_, s_, d_), BF16),
            jax.ShapeDtypeStruct((b_, s_ // LANES, N_EXPERTS, LANES), F32),
        ],
        compiler_params=_cparams(("parallel", "parallel")),
        name="outproj",
    )(oa, ob, x, wo_a, wo_b, g, b, rh, rl)


def _route_kernel(aff_ref, tri_ref, pos_ref, gate_ref, off_ref, *, cap):
    nc = aff_ref.shape[1]
    aff = aff_ref[0]
    bits = pltpu.bitcast(aff, I32)

    def count(mask):
        per_lane = jnp.sum(mask.astype(F32), axis=0)
        return jnp.sum(per_lane, axis=1, keepdims=True)

    thr = jnp.zeros((N_EXPERTS, 1), I32)
    for bit in range(30, -1, -1):
        cand = thr | (1 << bit)
        thr = jnp.where(count(bits >= cand[None]) >= cap, cand, thr)

    gt = bits > thr[None]
    eq = bits == thr[None]
    need = cap - count(gt)

    def prefix(mask, out_ref, extra_ref):
        mb = mask.astype(BF16).reshape(nc * N_EXPERTS, LANES)
        incl = _dot(mb, tri_ref[...]).reshape(nc, N_EXPERTS, LANES)
        excl = incl - mask.astype(F32)
        tot = incl[:, :, LANES - 1:LANES]

        def body(c, off):
            out_ref[0, c] = (excl[c] + off).astype(I32)
            if extra_ref is not None:
                extra_ref[0, c] = jnp.broadcast_to(off, (N_EXPERTS, LANES)).astype(I32)
            return off + tot[c]

        off = jnp.zeros((N_EXPERTS, 1), F32)
        for c in range(nc):
            off = body(c, off)

    prefix(eq, pos_ref, None)
    sel = gt | (eq & (pos_ref[0] < need[None].astype(I32)))
    prefix(sel, pos_ref, off_ref)
    pos_ref[0] = jnp.where(sel, pos_ref[0], -1)
    gate_ref[0] = jnp.where(sel, aff, 0.0)


def _route(aff, tri, cap):
    b_, nc, _, _ = aff.shape
    blk = pl.BlockSpec((1, nc, N_EXPERTS, LANES), lambda b: (b, 0, 0, 0))
    return pl.pallas_call(
        functools.partial(_route_kernel, cap=cap),
        grid=(b_,),
        in_specs=[blk, pl.BlockSpec((LANES, LANES), lambda b: (0, 0))],
        out_specs=[blk, blk, blk],
        out_shape=[jax.ShapeDtypeStruct(aff.shape, I32), jax.ShapeDtypeStruct(aff.shape, F32),
                   jax.ShapeDtypeStruct(aff.shape, I32)],
        compiler_params=_cparams(("parallel",)),
        name="route",
    )(aff, tri)


def _slab_geometry(starts_ref, b, j, e):
    start = starts_ref[b, j, e]
    count = starts_ref[b, j + 1, e] - start
    base = (start // BF16_ROWS) * BF16_ROWS
    nslab = (start - base + count + SLAB - 1) // SLAB
    return base, jnp.where(count > 0, nslab, 0)


def _dispatch_kernel(starts_ref, x_ref, pos_ref, xe_ref):
    b = pl.program_id(0)
    eg = pl.program_id(1)
    j = pl.program_id(2)
    t = x_ref.shape[1]

    @pl.when(j == 0)
    def _zero():
        xe_ref[...] = jnp.zeros(xe_ref.shape, BF16)

    xt = x_ref[0]
    rid = lax.broadcasted_iota(I32, (SLAB, t), 0)
    for el in range(E_GROUP):
        e = eg * E_GROUP + el
        base, nslab = _slab_geometry(starts_ref, b, j, e)
        prow = jnp.concatenate(
            [pos_ref[0, c, pl.ds(e, 1), :] for c in range(t // LANES)], axis=1)
        local = jnp.broadcast_to(prow - base, (SLAB, t))

        def body(k, carry, el=el, local=local, base=base):
            onehot = (local == rid + k * SLAB).astype(BF16)
            rows = _dot(onehot, xt).astype(BF16)
            dst = pl.ds(pl.multiple_of(base + k * SLAB, BF16_ROWS), SLAB)
            xe_ref[0, el, dst, :] = xe_ref[0, el, dst, :] + rows
            return carry

        lax.fori_loop(0, nslab, body, 0)


def _dispatch(starts, x1b, pos, capp):
    b_, s_, d_ = x1b.shape
    t = min(T_TOK, s_)
    grid_spec = pltpu.PrefetchScalarGridSpec(
        num_scalar_prefetch=1,
        grid=(b_, N_EXPERTS // E_GROUP, s_ // t),
        in_specs=[
            pl.BlockSpec((1, t, d_), lambda b, g, j, *_: (b, j, 0)),
            pl.BlockSpec((1, t // LANES, N_EXPERTS, LANES), lambda b, g, j, *_: (b, j, 0, 0)),
        ],
        out_specs=pl.BlockSpec((1, E_GROUP, capp, d_), lambda b, g, j, *_: (b, g, 0, 0)),
    )
    return pl.pallas_call(
        _dispatch_kernel,
        grid_spec=grid_spec,
        out_shape=jax.ShapeDtypeStruct((b_, N_EXPERTS, capp, d_), BF16),
        compiler_params=_cparams(("parallel", "parallel", "arbitrary")),
        name="dispatch",
    )(starts, x1b, pos)


def _ffn_kernel(xe_ref, wg_ref, wu_ref, wd_ref, y_ref, wg_s, wu_s, wd_s, acc_ref, *, cap):
    fc = pl.program_id(1)
    b = pl.program_id(2)
    nf = pl.num_programs(1)

    @pl.when(b == 0)
    def _cast_weights():
        wg_s[...] = wg_ref[0].astype(BF16)
        wu_s[...] = wu_ref[0].astype(BF16)
        wd_s[...] = wd_ref[0].astype(BF16)

    xe = xe_ref[0, 0]
    hg = _dot(xe, wg_s[...])
    hu = _dot(xe, wu_s[...])
    act = (hg * jax.nn.sigmoid(hg) * hu).astype(BF16)
    part = _dot(act, wd_s[...])

    @pl.when(fc == 0)
    def _first():
        acc_ref[b] = part

    @pl.when(fc > 0)
    def _rest():
        acc_ref[b] = acc_ref[b] + part

    @pl.when(fc == nf - 1)
    def _emit():
        y_ref[0, 0, 0:cap, :] = acc_ref[b].astype(BF16)
        y_ref[0, 0, cap:, :] = jnp.zeros((y_ref.shape[2] - cap, y_ref.shape[3]), BF16)


def _ffn(xe, w_gate, w_up, w_down, cap):
    b_, ne, capp, d_ = xe.shape
    f_ = w_gate.shape[2]
    fcw = min(FC, f_)
    nf = f_ // fcw
    return pl.pallas_call(
        functools.partial(_ffn_kernel, cap=cap),
        grid=(ne, nf, b_),
        in_specs=[
            pl.BlockSpec((1, 1, cap, d_), lambda e, f, b: (b, e, 0, 0)),
            pl.BlockSpec((1, d_, fcw), lambda e, f, b: (e, 0, f)),
            pl.BlockSpec((1, d_, fcw), lambda e, f, b: (e, 0, f)),
            pl.BlockSpec((1, fcw, d_), lambda e, f, b: (e, f, 0)),
        ],
        out_specs=pl.BlockSpec((1, 1, capp, d_), lambda e, f, b: (jnp.where(f == nf - 1, b, 0), e, 0, 0)),
        out_shape=jax.ShapeDtypeStruct((b_, ne, capp, d_), BF16),
        scratch_shapes=[pltpu.VMEM((d_, fcw), BF16), pltpu.VMEM((d_, fcw), BF16),
                        pltpu.VMEM((fcw, d_), BF16), pltpu.VMEM((b_, cap, d_), F32)],
        compiler_params=_cparams(("arbitrary", "arbitrary", "arbitrary")),
        name="ffn",
    )(xe, w_gate, w_up, w_down)


def _combine_kernel(starts_ref, y_hbm, pos_ref, gate_ref, x1_ref, p_ref, wpg_ref, wpp_ref,
                    g2_ref, b2_ref, g3_ref, b3_ref, o_ref, ybuf, sem, moe_ref, *, alpha):
    b = pl.program_id(0)
    j = pl.program_id(1)
    t = x1_ref.shape[1]

    def slab_copy(e, base, k, slot):
        src = y_hbm.at[b, e, pl.ds(pl.multiple_of(base + k * SLAB, BF16_ROWS), SLAB), :]
        return pltpu.make_async_copy(src, ybuf.at[slot], sem.at[slot])

    geo = [_slab_geometry(starts_ref, b, j, e) for e in range(N_EXPERTS)]
    for e, (base, nslab) in enumerate(geo):
        @pl.when(nslab > 0)
        def _start(e=e, base=base):
            slab_copy(e, base, 0, e).start()

    pad = jnp.zeros((LANES - N_EXPERTS, LANES), F32)
    pos_t = jnp.concatenate(
        [jnp.concatenate([pos_ref[0, c].astype(F32), pad], axis=0).T for c in range(t // LANES)], axis=0)
    gate_t = jnp.concatenate(
        [jnp.concatenate([gate_ref[0, c], pad], axis=0).T for c in range(t // LANES)], axis=0)
    cid = lax.broadcasted_iota(I32, (t, SLAB), 1).astype(F32)

    moe_ref[...] = jnp.zeros(moe_ref.shape, F32)
    for e, (base, nslab) in enumerate(geo):
        local = jnp.broadcast_to(pos_t[:, e:e + 1] - base.astype(F32), (t, SLAB))
        gate_col = gate_t[:, e:e + 1]

        def body(k, carry, e=e, base=base, local=local, gate_col=gate_col):
            @pl.when(k > 0)
            def _fetch_more():
                slab_copy(e, base, k, e).start()
            slab_copy(e, base, k, e).wait()
            onehot = (local == cid + (k * SLAB).astype(F32)).astype(BF16)
            moe_ref[...] = moe_ref[...] + gate_col * _dot(onehot, ybuf[e])
            return carry

        lax.fori_loop(0, nslab, body, 0)

    x2 = _layer_norm(alpha * x1_ref[0] + moe_ref[...], g2_ref[...], b2_ref[...])
    gate = jax.nn.sigmoid(_dot(x2.astype(BF16), wpg_ref[...]))
    ple = _dot(p_ref[0].astype(BF16), wpp_ref[...]) * gate
    o_ref[0] = _layer_norm(alpha * x2 + ple, g3_ref[...], b3_ref[...])


def _combine(starts, y, pos, gate, x1, p, wpg, wpp, g2, b2, g3, b3, alpha):
    b_, s_, d_ = x1.shape
    t = min(T_TOK, s_)
    pd = p.shape[2]
    full = lambda shape: pl.BlockSpec(shape, lambda b, j, *_: (0,) * len(shape))
    tile4 = pl.BlockSpec((1, t // LANES, N_EXPERTS, LANES), lambda b, j, *_: (b, j, 0, 0))
    grid_spec = pltpu.PrefetchScalarGridSpec(
        num_scalar_prefetch=1,
        grid=(b_, s_ // t),
        in_specs=[
            pl.BlockSpec(memory_space=pl.ANY),
            tile4, tile4,
            pl.BlockSpec((1, t, d_), lambda b, j, *_: (b, j, 0)),
            pl.BlockSpec((1, t, pd), lambda b, j, *_: (b, j, 0)),
            full((d_, d_)), full((pd, d_)),
            full((1, d_)), full((1, d_)), full((1, d_)), full((1, d_)),
        ],
        out_specs=pl.BlockSpec((1, t, d_), lambda b, j, *_: (b, j, 0)),
        scratch_shapes=[pltpu.VMEM((N_EXPERTS, SLAB, d_), BF16),
                        pltpu.SemaphoreType.DMA((N_EXPERTS,)),
                        pltpu.VMEM((t, d_), F32)],
    )
    return pl.pallas_call(
        functools.partial(_combine_kernel, alpha=alpha),
        grid_spec=grid_spec,
        out_shape=jax.ShapeDtypeStruct((b_, s_, d_), F32),
        compiler_params=_cparams(("parallel", "arbitrary")),
        name="combine",
    )(starts, y, pos, gate, x1, p, wpg, wpp, g2, b2, g3, b3)


def _rope_tables(s_):
    half = HEAD_DIM // 2
    inv = ROPE_THETA ** (-jnp.arange(0, half, 2, dtype=F32) / half)
    t = jnp.arange(s_, dtype=jnp.int32)
    row = (t // GRID_W).astype(F32)[:, None] * inv[None, :]
    col = (t % GRID_W).astype(F32)[:, None] * inv[None, :]
    cos = jnp.concatenate([jnp.cos(row), jnp.cos(row), jnp.cos(col), jnp.cos(col)], axis=1)
    sin = jnp.concatenate([-jnp.sin(row), jnp.sin(row), -jnp.sin(col), jnp.sin(col)], axis=1)
    reps = LANES // HEAD_DIM
    return jnp.tile(cos, (1, reps)), jnp.tile(sin, (1, reps))


def _t5_bucket(rel):
    half = NUM_BUCKETS // 2
    max_exact = half // 2
    ret = (rel > 0).astype(jnp.int32) * half
    n = jnp.abs(rel)
    nf = jnp.maximum(n, 1).astype(F32)
    large = max_exact + (jnp.log(nf / max_exact) / math.log(MAX_DISTANCE / max_exact)
                         * (half - max_exact)).astype(jnp.int32)
    large = jnp.minimum(large, half - 1)
    return ret + jnp.where(n < max_exact, n, large)


def _bucket_thresholds():
    half = NUM_BUCKETS // 2
    n = jnp.arange(0, MAX_DISTANCE + 1, dtype=jnp.int32)
    bk = _t5_bucket(-n)
    j = jnp.arange(half, dtype=jnp.int32)
    return jnp.sum((bk[None, :] < j[:, None]).astype(jnp.int32), axis=1)


def _group_mean_matrix(width):
    g = jnp.arange(width, dtype=jnp.int32) // HEAD_DIM
    return ((g[:, None] == g[None, :]).astype(F32) / HEAD_DIM).astype(BF16)


def kernel(x, p, w_in, w_out, a_q_norm, a_k_norm, b_lambda_q1, b_lambda_k1, b_lambda_q2, b_lambda_k2,
           b_subln, rel_bias, ln1_g, ln1_b, w_router, w_gate, w_up, w_down, ln2_g, ln2_b,
           w_ple_gate, w_ple_proj, ln3_g, ln3_b):
    b_, s_, d_ = x.shape
    depth = w_in.shape[0]
    alpha = (2 * depth) ** 0.25
    cap = EC_CAPACITY_FACTOR * s_ // N_EXPERTS
    t_tok = min(T_TOK, s_)
    capp = cap + t_tok + BF16_ROWS
    assert s_ % GRID_W == 0 and s_ % LANES == 0 and cap % BF16_ROWS == 0

    cs, sn = _rope_tables(s_)
    thr = _bucket_thresholds()
    bias_vals = (rel_bias.astype(F32) * LOG2E).T
    gmq, gmk = _group_mean_matrix(A_Q), _group_mean_matrix(A_KV)
    tri = (jnp.arange(LANES)[:, None] <= jnp.arange(LANES)[None, :]).astype(BF16)
    row = lambda v: v.astype(F32).reshape(1, -1)

    for i in range(depth):
        lam_init = 0.8 - 0.6 * math.exp(-0.3 * i)
        qa, kva, qb, kb, vb = _inproj(
            x, w_in[i].astype(BF16), cs, sn,
            jnp.tile(row(a_q_norm[i]), (1, A_HEADS)), jnp.tile(row(a_k_norm[i]), (1, A_KV_HEADS)), gmq, gmk)
        oa = _attn_a(qa, kva)
        ob = _attn_b(qb, kb, vb, thr, bias_vals, row(b_lambda_q1[i]), row(b_lambda_k1[i]),
                     row(b_lambda_q2[i]), row(b_lambda_k2[i]), row(b_subln[i]), lam_init)
        wr = jnp.pad(w_router[i].astype(F32), ((0, 0), (0, LANES - N_EXPERTS)))
        rh = wr.astype(BF16)
        rl = (wr - rh.astype(F32)).astype(BF16)
        wo = w_out[i].astype(BF16)
        x1, x1b, aff = _outproj(oa, ob, x, wo[:A_Q], wo[A_Q:], row(ln1_g[i]), row(ln1_b[i]), rh, rl, alpha)
        pos, gate, off = _route(aff, tri, cap)
        starts = jnp.concatenate(
            [off[:, ::t_tok // LANES, :, 0], jnp.full((b_, 1, N_EXPERTS), cap, jnp.int32)], axis=1)
        xe = _dispatch(starts, x1b, pos, capp)
        y = _ffn(xe, w_gate[i], w_up[i], w_down[i], cap)
        x = _combine(starts, y, pos, gate, x1, p[i], w_ple_gate[i].astype(BF16), w_ple_proj[i].astype(BF16),
                     row(ln2_g[i]), row(ln2_b[i]), row(ln3_g[i]), row(ln3_b[i]), alpha)
    return x
```

```python
import functools
import math

import jax
import jax.numpy as jnp
from jax import lax
from jax.experimental import pallas as pl
from jax.experimental.pallas import tpu as pltpu

F32 = jnp.float32
BF16 = jnp.bfloat16
I32 = jnp.int32

HEAD_DIM = 64
A_HEADS = 8
A_KV_HEADS = 2
A_REP = A_HEADS // A_KV_HEADS
B_HEADS = 4
B_VDIM = 2 * HEAD_DIM
A_Q = A_HEADS * HEAD_DIM
A_KV = A_KV_HEADS * HEAD_DIM
B_QK = B_HEADS * 2 * HEAD_DIM
B_WIDTH = B_HEADS * B_VDIM
ATTN_SCALE = HEAD_DIM ** -0.5
GRID_W = 64
ROPE_THETA = 10000.0
NUM_BUCKETS = 32
MAX_DISTANCE = 128
N_EXPERTS = 16
EC_CAPACITY_FACTOR = 2
LN_EPS = 1e-5
QK_EPS = 1e-6
LOG2E = math.log2(math.e)

LANES = 128
BF16_ROWS = 16
VMEM_LIMIT = 56 * 1024 * 1024

TM_PROJ = 512
TQ_A = 256
TK_A = 512
T_B = 512
BIAS_TILES = 5
T_TOK = 256
SLAB = 64
E_GROUP = 4
FC = 512
NEG_BIG = -1e30


def _cparams(sem):
    return pltpu.CompilerParams(dimension_semantics=sem, vmem_limit_bytes=VMEM_LIMIT)


def _dot(a, b):
    return jnp.dot(a, b, preferred_element_type=F32)


def _dot_nt(a, b):
    return lax.dot_general(a, b, (((1,), (1,)), ((), ())), preferred_element_type=F32)


def _layer_norm(y, g, b):
    mu = jnp.mean(y, axis=-1, keepdims=True)
    yc = y - mu
    var = jnp.mean(yc * yc, axis=-1, keepdims=True)
    return yc * lax.rsqrt(var + LN_EPS) * g + b


def _split_bf16(v):
    hi = v.astype(BF16)
    lo = (v - hi.astype(F32)).astype(BF16)
    return hi, lo


def _inproj_kernel(x_ref, w_ref, cs_ref, sn_ref, gq_ref, gk_ref, gmq_ref, gmk_ref,
                   qa_ref, kva_ref, kvat_ref, qb_ref, kb_ref, vbt_ref):
    tm = x_ref.shape[1]
    xb = x_ref[0].astype(BF16)
    lane = lax.broadcasted_iota(I32, (tm, LANES), 1)
    lo_half = lane < HEAD_DIM
    cs = cs_ref[...]
    sn = sn_ref[...]
    qscale = ATTN_SCALE * LOG2E

    def group_rms(v, gm_ref, gain):
        hi, lo = _split_bf16(v * v)
        ms = _dot(hi, gm_ref[...]) + _dot(lo, gm_ref[...])
        return v * lax.rsqrt(ms + QK_EPS) * gain

    def rope(v, reps):
        width = v.shape[1]
        lane_w = lax.broadcasted_iota(I32, v.shape, 1)
        first = (lane_w % (HEAD_DIM // 2)) < (HEAD_DIM // 4)
        rot = jnp.where(first, pltpu.roll(v, width - HEAD_DIM // 4, 1), pltpu.roll(v, HEAD_DIM // 4, 1))
        c = jnp.concatenate([cs] * reps, axis=1) if reps > 1 else cs
        s = jnp.concatenate([sn] * reps, axis=1) if reps > 1 else sn
        return v * c + rot * s

    qa = _dot(xb, w_ref[:, 0:A_Q])
    qa = rope(group_rms(qa, gmq_ref, gq_ref[...]), A_Q // LANES) * qscale
    for c in range(A_Q // LANES):
        chunk = qa[:, c * LANES:(c + 1) * LANES]
        qa_ref[0, 2 * c] = jnp.where(lo_half, chunk, 0.0).astype(BF16)
        qa_ref[0, 2 * c + 1] = jnp.where(lo_half, pltpu.roll(chunk, HEAD_DIM, 1), 0.0).astype(BF16)

    ka = _dot(xb, w_ref[:, A_Q:A_Q + A_KV])
    ka = rope(group_rms(ka, gmk_ref, gk_ref[...]), 1)
    va = _dot(xb, w_ref[:, A_Q + A_KV:A_Q + 2 * A_KV])
    for g, kv in enumerate((jnp.where(lo_half, ka, pltpu.roll(va, HEAD_DIM, 1)),
                            jnp.where(lo_half, pltpu.roll(ka, HEAD_DIM, 1), va))):
        kva_ref[0, g] = kv.astype(BF16)
        kvat_ref[0, g] = kv.T.astype(BF16)

    o = A_Q + 2 * A_KV
    qb = _dot(xb, w_ref[:, o:o + B_QK]) * qscale
    kb = _dot(xb, w_ref[:, o + B_QK:o + 2 * B_QK])
    vb = _dot(xb, w_ref[:, o + 2 * B_QK:o + 2 * B_QK + B_WIDTH])
    for h in range(B_HEADS):
        chunk = qb[:, h * LANES:(h + 1) * LANES]
        qb_ref[0, h, 0] = jnp.where(lo_half, chunk, 0.0).astype(BF16)
        qb_ref[0, h, 1] = jnp.where(lo_half, 0.0, chunk).astype(BF16)
        kb_ref[0, h] = kb[:, h * LANES:(h + 1) * LANES].astype(BF16)
        vbt_ref[0, h] = vb[:, h * LANES:(h + 1) * LANES].T.astype(BF16)


def _inproj(x, w_in_bf, cs, sn, gq, gk, gmq, gmk):
    b_, s_, d_ = x.shape
    tm = min(TM_PROJ, s_)
    nst = s_ // tm
    in_w = w_in_bf.shape[1]
    full = lambda shape: pl.BlockSpec(shape, lambda b, i: (0,) * len(shape))
    return pl.pallas_call(
        _inproj_kernel,
        grid=(b_, nst),
        in_specs=[
            pl.BlockSpec((1, tm, d_), lambda b, i: (b, i, 0)),
            full((d_, in_w)),
            pl.BlockSpec((tm, LANES), lambda b, i: (i, 0)),
            pl.BlockSpec((tm, LANES), lambda b, i: (i, 0)),
            full((1, A_Q)), full((1, A_KV)), full((A_Q, A_Q)), full((A_KV, A_KV)),
        ],
        out_specs=[
            pl.BlockSpec((1, A_HEADS, tm, LANES), lambda b, i: (b, 0, i, 0)),
            pl.BlockSpec((1, A_KV_HEADS, tm, LANES), lambda b, i: (b, 0, i, 0)),
            pl.BlockSpec((1, A_KV_HEADS, LANES, tm), lambda b, i: (b, 0, 0, i)),
            pl.BlockSpec((1, B_HEADS, 2, tm, LANES), lambda b, i: (b, 0, 0, i, 0)),
            pl.BlockSpec((1, B_HEADS, tm, LANES), lambda b, i: (b, 0, i, 0)),
            pl.BlockSpec((1, B_HEADS, B_VDIM, tm), lambda b, i: (b, 0, 0, i)),
        ],
        out_shape=[
            jax.ShapeDtypeStruct((b_, A_HEADS, s_, LANES), BF16),
            jax.ShapeDtypeStruct((b_, A_KV_HEADS, s_, LANES), BF16),
            jax.ShapeDtypeStruct((b_, A_KV_HEADS, LANES, s_), BF16),
            jax.ShapeDtypeStruct((b_, B_HEADS, 2, s_, LANES), BF16),
            jax.ShapeDtypeStruct((b_, B_HEADS, s_, LANES), BF16),
            jax.ShapeDtypeStruct((b_, B_HEADS, B_VDIM, s_), BF16),
        ],
        compiler_params=_cparams(("parallel", "parallel")),
        name="inproj",
    )(x, w_in_bf, cs, sn, gq, gk, gmq, gmk)


def _softmax_step(s, smax, vt_chunk, m_ref, l_ref, acc_ref):
    m_prev = m_ref[...]
    m_cur = jnp.maximum(m_prev, smax)
    alpha = jnp.exp2(m_prev - m_cur)
    p = jnp.exp2(s - m_cur)
    l_ref[...] = alpha * l_ref[...] + jnp.sum(p, axis=0, keepdims=True)
    acc_ref[...] = alpha * acc_ref[...] + _dot(vt_chunk, p.astype(BF16))
    m_ref[...] = m_cur


def _init_stats(m_ref, l_ref, acc_ref):
    m_ref[...] = jnp.full(m_ref.shape, NEG_BIG, F32)
    l_ref[...] = jnp.zeros(l_ref.shape, F32)
    acc_ref[...] = jnp.zeros(acc_ref.shape, F32)


def _pipelined_chunks(nk, scores, consume, s_ref, smax_ref):
    assert nk % 2 == 0

    def produce(j, slot):
        s = scores(j)
        s_ref[slot] = s
        smax_ref[slot] = jnp.max(s, axis=0, keepdims=True)

    def step(j, slot, prefetch):
        if prefetch:
            produce(j + 1, 1 - slot)
        consume(j, s_ref[slot], smax_ref[slot])

    def pair(i, carry):
        step(2 * i, 0, True)
        step(2 * i + 1, 1, True)
        return carry

    produce(0, 0)
    lax.fori_loop(0, nk // 2 - 1, pair, 0)
    step(nk - 2, 0, True)
    step(nk - 1, 1, False)


def _attn_a_kernel(q_ref, kv_ref, kvt_ref, o_ref, m_ref, l_ref, acc_ref, s_ref, smax_ref, *, tk):
    tq = q_ref.shape[2]
    cols = A_REP * tq
    nk = kv_ref.shape[2] // tk
    q = q_ref[0].reshape(cols, LANES)
    _init_stats(m_ref, l_ref, acc_ref)

    def scores(j):
        return _dot_nt(kv_ref[0, 0, pl.ds(pl.multiple_of(j * tk, tk), tk), :], q)

    def consume(j, s, smax):
        vt = kvt_ref[0, 0, :, pl.ds(pl.multiple_of(j * tk, tk), tk)]
        _softmax_step(s, smax, vt, m_ref, l_ref, acc_ref)

    _pipelined_chunks(nk, scores, consume, s_ref, smax_ref)
    o = acc_ref[HEAD_DIM:, :] / l_ref[...]
    o = jnp.concatenate([o[:, r * tq:(r + 1) * tq] for r in range(A_REP)], axis=0)
    o_ref[0] = o.T.astype(BF16)


def _attn_a(qa, kva, kvat):
    b_, _, s_, _ = qa.shape
    tq = min(TQ_A, s_)
    tk = min(TK_A, s_)
    cols = A_REP * tq
    return pl.pallas_call(
        functools.partial(_attn_a_kernel, tk=tk),
        grid=(b_, A_KV_HEADS, s_ // tq),
        in_specs=[
            pl.BlockSpec((1, A_REP, tq, LANES), lambda b, g, i: (b, g, i, 0)),
            pl.BlockSpec((1, 1, s_, LANES), lambda b, g, i: (b, g, 0, 0)),
            pl.BlockSpec((1, 1, LANES, s_), lambda b, g, i: (b, g, 0, 0)),
        ],
        out_specs=pl.BlockSpec((1, tq, A_REP * HEAD_DIM), lambda b, g, i: (b, i, g)),
        out_shape=jax.ShapeDtypeStruct((b_, s_, A_Q), BF16),
        scratch_shapes=[pltpu.VMEM((1, cols), F32), pltpu.VMEM((1, cols), F32),
                        pltpu.VMEM((LANES, cols), F32),
                        pltpu.VMEM((2, tk, cols), F32), pltpu.VMEM((2, 1, cols), F32)],
        compiler_params=_cparams(("parallel", "parallel", "arbitrary")),
        name="attn_a",
    )(qa, kva, kvat)


def _attn_b_kernel(thr_ref, val_ref, q_ref, k_ref, vt_ref, lq1_ref, lk1_ref, lq2_ref, lk2_ref, sub_ref,
                   o_ref, bias_ref, m_ref, l_ref, acc_ref, s_ref, smax_ref, *, lam_init):
    h = pl.program_id(1)
    qi = pl.program_id(2)
    t = q_ref.shape[3]
    nk = k_ref.shape[2] // t
    half = NUM_BUCKETS // 2
    reach = (BIAS_TILES - 1) // 2

    @pl.when(qi == 0)
    def _build_bias():
        key = lax.broadcasted_iota(I32, (t, t), 0)
        qry = lax.broadcasted_iota(I32, (t, t), 1)
        for d in range(-reach, reach + 1):
            rel = key - qry + d * t
            n = jnp.abs(rel)
            neg = jnp.full((t, t), val_ref[h, 0], F32)
            pos = jnp.full((t, t), val_ref[h, half], F32)
            for j in range(1, half):
                ge = n >= thr_ref[j]
                neg = jnp.where(ge, val_ref[h, j], neg)
                pos = jnp.where(ge, val_ref[h, half + j], pos)
            bias_ref[d + reach] = jnp.where(rel > 0, pos, neg)

    q = q_ref[0, 0].reshape(2 * t, LANES)
    _init_stats(m_ref, l_ref, acc_ref)

    def scores(j):
        bias = bias_ref[jnp.clip(j - qi, -reach, reach) + reach]
        kc = k_ref[0, 0, pl.ds(pl.multiple_of(j * t, t), t), :]
        return _dot_nt(kc, q) + jnp.concatenate([bias, bias], axis=1)

    def consume(j, s, smax):
        vt = vt_ref[0, 0, :, pl.ds(pl.multiple_of(j * t, t), t)]
        _softmax_step(s, smax, vt, m_ref, l_ref, acc_ref)

    _pipelined_chunks(nk, scores, consume, s_ref, smax_ref)

    lam = (jnp.exp(jnp.sum(lq1_ref[...] * lk1_ref[...], axis=1, keepdims=True))
           - jnp.exp(jnp.sum(lq2_ref[...] * lk2_ref[...], axis=1, keepdims=True)) + lam_init)
    o = acc_ref[...] / l_ref[...]
    o = o[:, 0:t] - lam * o[:, t:2 * t]
    ms = jnp.mean(o * o, axis=0, keepdims=True)
    o = o * lax.rsqrt(ms + LN_EPS) * sub_ref[...] * (1.0 - lam_init)
    o_ref[0] = o.T.astype(BF16)


def _attn_b(qb, kb, vbt, thr, vals, lq1, lk1, lq2, lk2, subln_col, lam_init):
    b_, _, _, s_, _ = qb.shape
    t = min(T_B, s_)
    assert t >= MAX_DISTANCE
    vec = lambda n: pl.BlockSpec((1, n), lambda b, h, i, *_: (0, 0))
    grid_spec = pltpu.PrefetchScalarGridSpec(
        num_scalar_prefetch=2,
        grid=(b_, B_HEADS, s_ // t),
        in_specs=[
            pl.BlockSpec((1, 1, 2, t, LANES), lambda b, h, i, *_: (b, h, 0, i, 0)),
            pl.BlockSpec((1, 1, s_, LANES), lambda b, h, i, *_: (b, h, 0, 0)),
            pl.BlockSpec((1, 1, B_VDIM, s_), lambda b, h, i, *_: (b, h, 0, 0)),
            vec(HEAD_DIM), vec(HEAD_DIM), vec(HEAD_DIM), vec(HEAD_DIM),
            pl.BlockSpec((B_VDIM, 1), lambda b, h, i, *_: (0, 0)),
        ],
        out_specs=pl.BlockSpec((1, t, B_VDIM), lambda b, h, i, *_: (b, i, h)),
        scratch_shapes=[pltpu.VMEM((BIAS_TILES, t, t), F32), pltpu.VMEM((1, 2 * t), F32),
                        pltpu.VMEM((1, 2 * t), F32), pltpu.VMEM((B_VDIM, 2 * t), F32),
                        pltpu.VMEM((2, t, 2 * t), F32), pltpu.VMEM((2, 1, 2 * t), F32)],
    )
    return pl.pallas_call(
        functools.partial(_attn_b_kernel, lam_init=lam_init),
        grid_spec=grid_spec,
        out_shape=jax.ShapeDtypeStruct((b_, s_, B_WIDTH), BF16),
        compiler_params=_cparams(("parallel", "parallel", "arbitrary")),
        name="attn_b",
    )(thr, vals, qb, kb, vbt, lq1, lk1, lq2, lk2, subln_col)


def _outproj_kernel(oa_ref, ob_ref, x_ref, wa_ref, wb_ref, g_ref, b_ref, rh_ref, rl_ref,
                    x1_ref, x1b_ref, aff_ref, *, alpha):
    tm = x_ref.shape[1]
    mix = _dot(oa_ref[0], wa_ref[...]) + _dot(ob_ref[0], wb_ref[...])
    x1 = _layer_norm(alpha * x_ref[0] + mix, g_ref[...], b_ref[...])
    x1_ref[0] = x1
    x1b_ref[0] = x1.astype(BF16)
    hi, lo = _split_bf16(x1)
    logits = _dot(hi, rh_ref[...]) + _dot(lo, rh_ref[...]) + _dot(hi, rl_ref[...])
    lane = lax.broadcasted_iota(I32, (tm, LANES), 1)
    logits = jnp.where(lane < N_EXPERTS, logits, NEG_BIG)
    e = jnp.exp(logits - jnp.max(logits, axis=1, keepdims=True))
    aff = e / jnp.sum(e, axis=1, keepdims=True)
    aff_t = aff.T
    for c in range(tm // LANES):
        aff_ref[0, c] = aff_t[0:N_EXPERTS, c * LANES:(c + 1) * LANES]


def _outproj(oa, ob, x, wo_a, wo_b, g, b, rh, rl, alpha):
    b_, s_, d_ = x.shape
    tm = min(TM_PROJ, s_)
    nch = tm // LANES
    full = lambda shape: pl.BlockSpec(shape, lambda b, i: (0,) * len(shape))
    return pl.pallas_call(
        functools.partial(_outproj_kernel, alpha=alpha),
        grid=(b_, s_ // tm),
        in_specs=[
            pl.BlockSpec((1, tm, A_Q), lambda b, i: (b, i, 0)),
            pl.BlockSpec((1, tm, B_WIDTH), lambda b, i: (b, i, 0)),
            pl.BlockSpec((1, tm, d_), lambda b, i: (b, i, 0)),
            full((A_Q, d_)), full((B_WIDTH, d_)), full((1, d_)), full((1, d_)),
            full((d_, LANES)), full((d_, LANES)),
        ],
        out_specs=[
            pl.BlockSpec((1, tm, d_), lambda b, i: (b, i, 0)),
            pl.BlockSpec((1, tm, d_), lambda b, i: (b, i, 0)),
            pl.BlockSpec((1, nch, N_EXPERTS, LANES), lambda b, i: (b, i, 0, 0)),
        ],
        out_shape=[
            jax.ShapeDtypeStruct((b_, s_, d_), F32),
            jax.ShapeDtypeStruct((b_, s_, d_), BF16),
            jax.ShapeDtypeStruct((b_, s_ // LANES, N_EXPERTS, LANES), F32),
        ],
        compiler_params=_cparams(("parallel", "parallel")),
        name="outproj",
    )(oa, ob, x, wo_a, wo_b, g, b, rh, rl)


def _route_kernel(aff_ref, tri_ref, pos_ref, gate_ref, off_ref, *, cap):
    nc = aff_ref.shape[1]
    aff = aff_ref[0]
    bits = pltpu.bitcast(aff, I32)

    def count(mask):
        per_lane = jnp.sum(mask.astype(F32), axis=0)
        return jnp.sum(per_lane, axis=1, keepdims=True)

    thr = jnp.zeros((N_EXPERTS, 1), I32)
    for bit in range(30, -1, -1):
        cand = thr | (1 << bit)
        thr = jnp.where(count(bits >= cand[None]) >= cap, cand, thr)

    gt = bits > thr[None]
    eq = bits == thr[None]
    need = cap - count(gt)

    def prefix(mask, out_ref, extra_ref):
        mb = mask.astype(BF16).reshape(nc * N_EXPERTS, LANES)
        incl = _dot(mb, tri_ref[...]).reshape(nc, N_EXPERTS, LANES)
        excl = incl - mask.astype(F32)
        tot = incl[:, :, LANES - 1:LANES]

        def body(c, off):
            out_ref[0, c] = (excl[c] + off).astype(I32)
            if extra_ref is not None:
                extra_ref[0, c] = jnp.broadcast_to(off, (N_EXPERTS, LANES)).astype(I32)
            return off + tot[c]

        off = jnp.zeros((N_EXPERTS, 1), F32)
        for c in range(nc):
            off = body(c, off)

    prefix(eq, pos_ref, None)
    sel = gt | (eq & (pos_ref[0] < need[None].astype(I32)))
    prefix(sel, pos_ref, off_ref)
    pos_ref[0] = jnp.where(sel, pos_ref[0], -1)
    gate_ref[0] = jnp.where(sel, aff, 0.0)


def _route(aff, tri, cap):
    b_, nc, _, _ = aff.shape
    blk = pl.BlockSpec((1, nc, N_EXPERTS, LANES), lambda b: (b, 0, 0, 0))
    return pl.pallas_call(
        functools.partial(_route_kernel, cap=cap),
        grid=(b_,),
        in_specs=[blk, pl.BlockSpec((LANES, LANES), lambda b: (0, 0))],
        out_specs=[blk, blk, blk],
        out_shape=[jax.ShapeDtypeStruct(aff.shape, I32), jax.ShapeDtypeStruct(aff.shape, F32),
                   jax.ShapeDtypeStruct(aff.shape, I32)],
        compiler_params=_cparams(("parallel",)),
        name="route",
    )(aff, tri)


def _slab_geometry(starts_ref, b, j, e):
    start = starts_ref[b, j, e]
    count = starts_ref[b, j + 1, e] - start
    base = (start // BF16_ROWS) * BF16_ROWS
    nslab = (start - base + count + SLAB - 1) // SLAB
    return base, jnp.where(count > 0, nslab, 0)


def _dispatch_kernel(starts_ref, x_ref, pos_ref, xe_ref):
    b = pl.program_id(0)
    eg = pl.program_id(1)
    j = pl.program_id(2)
    t = x_ref.shape[1]

    @pl.when(j == 0)
    def _zero():
        xe_ref[...] = jnp.zeros(xe_ref.shape, BF16)

    xt = x_ref[0]
    rid = lax.broadcasted_iota(I32, (SLAB, t), 0)
    for el in range(E_GROUP):
        e = eg * E_GROUP + el
        base, nslab = _slab_geometry(starts_ref, b, j, e)
        prow = jnp.concatenate(
            [pos_ref[0, c, pl.ds(e, 1), :] for c in range(t // LANES)], axis=1)
        local = jnp.broadcast_to(prow - base, (SLAB, t))

        def body(k, carry, el=el, local=local, base=base):
            onehot = (local == rid + k * SLAB).astype(BF16)
            rows = _dot(onehot, xt).astype(BF16)
            dst = pl.ds(pl.multiple_of(base + k * SLAB, BF16_ROWS), SLAB)
            xe_ref[0, el, dst, :] = xe_ref[0, el, dst, :] + rows
            return carry

        lax.fori_loop(0, nslab, body, 0)


def _dispatch(starts, x1b, pos, capp):
    b_, s_, d_ = x1b.shape
    t = min(T_TOK, s_)
    grid_spec = pltpu.PrefetchScalarGridSpec(
        num_scalar_prefetch=1,
        grid=(b_, N_EXPERTS // E_GROUP, s_ // t),
        in_specs=[
            pl.BlockSpec((1, t, d_), lambda b, g, j, *_: (b, j, 0)),
            pl.BlockSpec((1, t // LANES, N_EXPERTS, LANES), lambda b, g, j, *_: (b, j, 0, 0)),
        ],
        out_specs=pl.BlockSpec((1, E_GROUP, capp, d_), lambda b, g, j, *_: (b, g, 0, 0)),
    )
    return pl.pallas_call(
        _dispatch_kernel,
        grid_spec=grid_spec,
        out_shape=jax.ShapeDtypeStruct((b_, N_EXPERTS, capp, d_), BF16),
        compiler_params=_cparams(("parallel", "parallel", "arbitrary")),
        name="dispatch",
    )(starts, x1b, pos)


def _ffn_kernel(xe_ref, wg_ref, wu_ref, wd_ref, y_ref, wg_s, wu_s, wd_s, acc_ref, *, cap):
    fc = pl.program_id(1)
    b = pl.program_id(2)
    nf = pl.num_programs(1)

    @pl.when(b == 0)
    def _cast_weights():
        wg_s[...] = wg_ref[0].astype(BF16)
        wu_s[...] = wu_ref[0].astype(BF16)
        wd_s[...] = wd_ref[0].astype(BF16)

    xe = xe_ref[0, 0]
    hg = _dot(xe, wg_s[...])
    hu = _dot(xe, wu_s[...])
    act = (hg * jax.nn.sigmoid(hg) * hu).astype(BF16)
    part = _dot(act, wd_s[...])

    @pl.when(fc == 0)
    def _first():
        acc_ref[b] = part

    @pl.when(fc > 0)
    def _rest():
        acc_ref[b] = acc_ref[b] + part

    @pl.when(fc == nf - 1)
    def _emit():
        y_ref[0, 0, 0:cap, :] = acc_ref[b].astype(BF16)
        y_ref[0, 0, cap:, :] = jnp.zeros((y_ref.shape[2] - cap, y_ref.shape[3]), BF16)


def _ffn(xe, w_gate, w_up, w_down, cap):
    b_, ne, capp, d_ = xe.shape
    f_ = w_gate.shape[2]
    fcw = min(FC, f_)
    nf = f_ // fcw
    return pl.pallas_call(
        functools.partial(_ffn_kernel, cap=cap),
        grid=(ne, nf, b_),
        in_specs=[
            pl.BlockSpec((1, 1, cap, d_), lambda e, f, b: (b, e, 0, 0)),
            pl.BlockSpec((1, d_, fcw), lambda e, f, b: (e, 0, f)),
            pl.BlockSpec((1, d_, fcw), lambda e, f, b: (e, 0, f)),
            pl.BlockSpec((1, fcw, d_), lambda e, f, b: (e, f, 0)),
        ],
        out_specs=pl.BlockSpec((1, 1, capp, d_), lambda e, f, b: (jnp.where(f == nf - 1, b, 0), e, 0, 0)),
        out_shape=jax.ShapeDtypeStruct((b_, ne, capp, d_), BF16),
        scratch_shapes=[pltpu.VMEM((d_, fcw), BF16), pltpu.VMEM((d_, fcw), BF16),
                        pltpu.VMEM((fcw, d_), BF16), pltpu.VMEM((b_, cap, d_), F32)],
        compiler_params=_cparams(("arbitrary", "arbitrary", "arbitrary")),
        name="ffn",
    )(xe, w_gate, w_up, w_down)


def _combine_kernel(starts_ref, y_hbm, pos_ref, gate_ref, x1_ref, p_ref, wpg_ref, wpp_ref,
                    g2_ref, b2_ref, g3_ref, b3_ref, o_ref, ybuf, sem, moe_ref, *, alpha):
    b = pl.program_id(0)
    j = pl.program_id(1)
    t = x1_ref.shape[1]

    def slab_copy(e, base, k, slot):
        src = y_hbm.at[b, e, pl.ds(pl.multiple_of(base + k * SLAB, BF16_ROWS), SLAB), :]
        return pltpu.make_async_copy(src, ybuf.at[slot], sem.at[slot])

    geo = [_slab_geometry(starts_ref, b, j, e) for e in range(N_EXPERTS)]
    for e, (base, nslab) in enumerate(geo):
        @pl.when(nslab > 0)
        def _start(e=e, base=base):
            slab_copy(e, base, 0, e).start()

    pad = jnp.zeros((LANES - N_EXPERTS, LANES), F32)
    pos_t = jnp.concatenate(
        [jnp.concatenate([pos_ref[0, c].astype(F32), pad], axis=0).T for c in range(t // LANES)], axis=0)
    gate_t = jnp.concatenate(
        [jnp.concatenate([gate_ref[0, c], pad], axis=0).T for c in range(t // LANES)], axis=0)
    cid = lax.broadcasted_iota(I32, (t, SLAB), 1).astype(F32)

    moe_ref[...] = jnp.zeros(moe_ref.shape, F32)
    for e, (base, nslab) in enumerate(geo):
        local = jnp.broadcast_to(pos_t[:, e:e + 1] - base.astype(F32), (t, SLAB))
        gate_col = gate_t[:, e:e + 1]

        def body(k, carry, e=e, base=base, local=local, gate_col=gate_col):
            @pl.when(k > 0)
            def _fetch_more():
                slab_copy(e, base, k, e).start()
            slab_copy(e, base, k, e).wait()
            onehot = (local == cid + (k * SLAB).astype(F32)).astype(BF16)
            moe_ref[...] = moe_ref[...] + gate_col * _dot(onehot, ybuf[e])
            return carry

        lax.fori_loop(0, nslab, body, 0)

    x2 = _layer_norm(alpha * x1_ref[0] + moe_ref[...], g2_ref[...], b2_ref[...])
    gate = jax.nn.sigmoid(_dot(x2.astype(BF16), wpg_ref[...]))
    ple = _dot(p_ref[0].astype(BF16), wpp_ref[...]) * gate
    o_ref[0] = _layer_norm(alpha * x2 + ple, g3_ref[...], b3_ref[...])


def _combine(starts, y, pos, gate, x1, p, wpg, wpp, g2, b2, g3, b3, alpha):
    b_, s_, d_ = x1.shape
    t = min(T_TOK, s_)
    pd = p.shape[2]
    full = lambda shape: pl.BlockSpec(shape, lambda b, j, *_: (0,) * len(shape))
    tile4 = pl.BlockSpec((1, t // LANES, N_EXPERTS, LANES), lambda b, j, *_: (b, j, 0, 0))
    grid_spec = pltpu.PrefetchScalarGridSpec(
        num_scalar_prefetch=1,
        grid=(b_, s_ // t),
        in_specs=[
            pl.BlockSpec(memory_space=pl.ANY),
            tile4, tile4,
            pl.BlockSpec((1, t, d_), lambda b, j, *_: (b, j, 0)),
            pl.BlockSpec((1, t, pd), lambda b, j, *_: (b, j, 0)),
            full((d_, d_)), full((pd, d_)),
            full((1, d_)), full((1, d_)), full((1, d_)), full((1, d_)),
        ],
        out_specs=pl.BlockSpec((1, t, d_), lambda b, j, *_: (b, j, 0)),
        scratch_shapes=[pltpu.VMEM((N_EXPERTS, SLAB, d_), BF16),
                        pltpu.SemaphoreType.DMA((N_EXPERTS,)),
                        pltpu.VMEM((t, d_), F32)],
    )
    return pl.pallas_call(
        functools.partial(_combine_kernel, alpha=alpha),
        grid_spec=grid_spec,
        out_shape=jax.ShapeDtypeStruct((b_, s_, d_), F32),
        compiler_params=_cparams(("parallel", "arbitrary")),
        name="combine",
    )(starts, y, pos, gate, x1, p, wpg, wpp, g2, b2, g3, b3)


def _rope_tables(s_):
    half = HEAD_DIM // 2
    inv = ROPE_THETA ** (-jnp.arange(0, half, 2, dtype=F32) / half)
    t = jnp.arange(s_, dtype=jnp.int32)
    row = (t // GRID_W).astype(F32)[:, None] * inv[None, :]
    col = (t % GRID_W).astype(F32)[:, None] * inv[None, :]
    cos = jnp.concatenate([jnp.cos(row), jnp.cos(row), jnp.cos(col), jnp.cos(col)], axis=1)
    sin = jnp.concatenate([-jnp.sin(row), jnp.sin(row), -jnp.sin(col), jnp.sin(col)], axis=1)
    reps = LANES // HEAD_DIM
    return jnp.tile(cos, (1, reps)), jnp.tile(sin, (1, reps))


def _t5_bucket(rel):
    half = NUM_BUCKETS // 2
    max_exact = half // 2
    ret = (rel > 0).astype(jnp.int32) * half
    n = jnp.abs(rel)
    nf = jnp.maximum(n, 1).astype(F32)
    large = max_exact + (jnp.log(nf / max_exact) / math.log(MAX_DISTANCE / max_exact)
                         * (half - max_exact)).astype(jnp.int32)
    large = jnp.minimum(large, half - 1)
    return ret + jnp.where(n < max_exact, n, large)


def _bucket_thresholds():
    half = NUM_BUCKETS // 2
    n = jnp.arange(0, MAX_DISTANCE + 1, dtype=jnp.int32)
    bk = _t5_bucket(-n)
    j = jnp.arange(half, dtype=jnp.int32)
    return jnp.sum((bk[None, :] < j[:, None]).astype(jnp.int32), axis=1)


def _group_mean_matrix(width):
    g = jnp.arange(width, dtype=jnp.int32) // HEAD_DIM
    return ((g[:, None] == g[None, :]).astype(F32) / HEAD_DIM).astype(BF16)


def kernel(x, p, w_in, w_out, a_q_norm, a_k_norm, b_lambda_q1, b_lambda_k1, b_lambda_q2, b_lambda_k2,
           b_subln, rel_bias, ln1_g, ln1_b, w_router, w_gate, w_up, w_down, ln2_g, ln2_b,
           w_ple_gate, w_ple_proj, ln3_g, ln3_b):
    b_, s_, d_ = x.shape
    depth = w_in.shape[0]
    alpha = (2 * depth) ** 0.25
    cap = EC_CAPACITY_FACTOR * s_ // N_EXPERTS
    t_tok = min(T_TOK, s_)
    capp = cap + t_tok + BF16_ROWS
    assert s_ % GRID_W == 0 and s_ % LANES == 0 and cap % BF16_ROWS == 0

    cs, sn = _rope_tables(s_)
    thr = _bucket_thresholds()
    bias_vals = (rel_bias.astype(F32) * LOG2E).T
    gmq, gmk = _group_mean_matrix(A_Q), _group_mean_matrix(A_KV)
    tri = (jnp.arange(LANES)[:, None] <= jnp.arange(LANES)[None, :]).astype(BF16)
    row = lambda v: v.astype(F32).reshape(1, -1)

    for i in range(depth):
        lam_init = 0.8 - 0.6 * math.exp(-0.3 * i)
        qa, kva, kvat, qb, kb, vbt = _inproj(
            x, w_in[i].astype(BF16), cs, sn,
            jnp.tile(row(a_q_norm[i]), (1, A_HEADS)), jnp.tile(row(a_k_norm[i]), (1, A_KV_HEADS)), gmq, gmk)
        oa = _attn_a(qa, kva, kvat)
        ob = _attn_b(qb, kb, vbt, thr, bias_vals, row(b_lambda_q1[i]), row(b_lambda_k1[i]),
                     row(b_lambda_q2[i]), row(b_lambda_k2[i]), b_subln[i].astype(F32).reshape(-1, 1), lam_init)
        wr = jnp.pad(w_router[i].astype(F32), ((0, 0), (0, LANES - N_EXPERTS)))
        rh = wr.astype(BF16)
        rl = (wr - rh.astype(F32)).astype(BF16)
        wo = w_out[i].astype(BF16)
        x1, x1b, aff = _outproj(oa, ob, x, wo[:A_Q], wo[A_Q:], row(ln1_g[i]), row(ln1_b[i]), rh, rl, alpha)
        pos, gate, off = _route(aff, tri, cap)
        starts = jnp.concatenate(
            [off[:, ::t_tok // LANES, :, 0], jnp.full((b_, 1, N_EXPERTS), cap, jnp.int32)], axis=1)
        xe = _dispatch(starts, x1b, pos, capp)
        y = _ffn(xe, w_gate[i], w_up[i], w_down[i], cap)
        x = _combine(starts, y, pos, gate, x1, p[i], w_ple_gate[i].astype(BF16), w_ple_proj[i].astype(BF16),
                     row(ln2_g[i]), row(ln2_b[i]), row(ln3_g[i]), row(ln3_b[i]), alpha)
    return x
```

```python
import functools
import math

import jax
import jax.numpy as jnp
from jax import lax
from jax.experimental import pallas as pl
from jax.experimental.pallas import tpu as pltpu

F32 = jnp.float32
BF16 = jnp.bfloat16
I32 = jnp.int32

HEAD_DIM = 64
A_HEADS = 8
A_KV_HEADS = 2
A_REP = A_HEADS // A_KV_HEADS
B_HEADS = 4
B_VDIM = 2 * HEAD_DIM
A_Q = A_HEADS * HEAD_DIM
A_KV = A_KV_HEADS * HEAD_DIM
B_QK = B_HEADS * 2 * HEAD_DIM
B_WIDTH = B_HEADS * B_VDIM
ATTN_SCALE = HEAD_DIM ** -0.5
GRID_W = 64
ROPE_THETA = 10000.0
NUM_BUCKETS = 32
MAX_DISTANCE = 128
N_EXPERTS = 16
EC_CAPACITY_FACTOR = 2
LN_EPS = 1e-5
QK_EPS = 1e-6
LOG2E = math.log2(math.e)

LANES = 128
BF16_ROWS = 16
ONES_ROWS = BF16_ROWS
VMEM_LIMIT = 56 * 1024 * 1024

TM_PROJ = 512
TQ_A = 256
TK_A = 512
T_B = 512
CHUNKS_PER_TRIP = 4
BIAS_TILES = 5
T_TOK = 256
SLAB = 64
E_GROUP = 4
FC = 512
FC_SUB = 256
NEG_BIG = -1e30


def _cparams(sem):
    return pltpu.CompilerParams(dimension_semantics=sem, vmem_limit_bytes=VMEM_LIMIT)


def _dot(a, b):
    return jnp.dot(a, b, preferred_element_type=F32)


def _dot_nt(a, b):
    return lax.dot_general(a, b, (((1,), (1,)), ((), ())), preferred_element_type=F32)


def _layer_norm(y, g, b):
    mu = jnp.mean(y, axis=-1, keepdims=True)
    yc = y - mu
    var = jnp.mean(yc * yc, axis=-1, keepdims=True)
    return yc * lax.rsqrt(var + LN_EPS) * g + b


def _split_bf16(v):
    hi = v.astype(BF16)
    lo = (v - hi.astype(F32)).astype(BF16)
    return hi, lo


def _inproj_kernel(x_ref, w_ref, wvt_ref, cs_ref, sn_ref, gq_ref, gk_ref, gmq_ref, gmk_ref,
                   qa_ref, ka_ref, vat_ref, qb_ref, kb_ref, vbt_ref):
    tm = x_ref.shape[1]
    xb = x_ref[0].astype(BF16)
    lane = lax.broadcasted_iota(I32, (tm, LANES), 1)
    lo_half = lane < HEAD_DIM
    cs = cs_ref[...]
    sn = sn_ref[...]
    qscale = ATTN_SCALE * LOG2E
    ones_rows = (lax.broadcasted_iota(I32, (ONES_ROWS, tm), 0) == 0).astype(BF16)

    def group_rms(v, gm_ref, gain):
        hi, lo = _split_bf16(v * v)
        ms = _dot(hi, gm_ref[...]) + _dot(lo, gm_ref[...])
        return v * lax.rsqrt(ms + QK_EPS) * gain

    def rope(v, reps):
        width = v.shape[1]
        lane_w = lax.broadcasted_iota(I32, v.shape, 1)
        first = (lane_w % (HEAD_DIM // 2)) < (HEAD_DIM // 4)
        rot = jnp.where(first, pltpu.roll(v, width - HEAD_DIM // 4, 1), pltpu.roll(v, HEAD_DIM // 4, 1))
        c = jnp.concatenate([cs] * reps, axis=1) if reps > 1 else cs
        s = jnp.concatenate([sn] * reps, axis=1) if reps > 1 else sn
        return v * c + rot * s

    qa = _dot(xb, w_ref[:, 0:A_Q])
    qa = rope(group_rms(qa, gmq_ref, gq_ref[...]), A_Q // LANES) * qscale
    for c in range(A_Q // LANES):
        chunk = qa[:, c * LANES:(c + 1) * LANES]
        swapped = pltpu.roll(chunk, HEAD_DIM, 1)
        if (2 * c) // A_REP == 0:
            even, odd = jnp.where(lo_half, chunk, 0.0), jnp.where(lo_half, swapped, 0.0)
        else:
            even, odd = jnp.where(lo_half, 0.0, swapped), jnp.where(lo_half, 0.0, chunk)
        qa_ref[0, 2 * c] = even.astype(BF16)
        qa_ref[0, 2 * c + 1] = odd.astype(BF16)

    ka = _dot(xb, w_ref[:, A_Q:A_Q + A_KV])
    ka_ref[0] = rope(group_rms(ka, gmk_ref, gk_ref[...]), 1).astype(BF16)
    v_t = _dot_nt(wvt_ref[...], xb).astype(BF16)
    for g in range(A_KV_HEADS):
        vat_ref[0, g] = jnp.concatenate([v_t[g * HEAD_DIM:(g + 1) * HEAD_DIM], ones_rows], axis=0)

    o = A_Q + 2 * A_KV
    qb = _dot(xb, w_ref[:, o:o + B_QK]) * qscale
    kb = _dot(xb, w_ref[:, o + B_QK:o + 2 * B_QK])
    for h in range(B_HEADS):
        chunk = qb[:, h * LANES:(h + 1) * LANES]
        qb_ref[0, h, 0] = jnp.where(lo_half, chunk, 0.0).astype(BF16)
        qb_ref[0, h, 1] = jnp.where(lo_half, 0.0, chunk).astype(BF16)
        kb_ref[0, h] = kb[:, h * LANES:(h + 1) * LANES].astype(BF16)
        vbt_ref[0, h] = jnp.concatenate(
            [v_t[A_KV + h * B_VDIM:A_KV + (h + 1) * B_VDIM], ones_rows], axis=0)


def _inproj(x, w_in_bf, cs, sn, gq, gk, gmq, gmk):
    b_, s_, d_ = x.shape
    tm = min(TM_PROJ, s_)
    nst = s_ // tm
    in_w = w_in_bf.shape[1]
    vb0 = A_Q + 2 * A_KV + 2 * B_QK
    wv_t = jnp.concatenate([w_in_bf[:, A_Q + A_KV:A_Q + 2 * A_KV], w_in_bf[:, vb0:vb0 + B_WIDTH]], axis=1).T
    full = lambda shape: pl.BlockSpec(shape, lambda b, i: (0,) * len(shape))
    return pl.pallas_call(
        _inproj_kernel,
        grid=(b_, nst),
        in_specs=[
            pl.BlockSpec((1, tm, d_), lambda b, i: (b, i, 0)),
            full((d_, in_w)),
            full((A_KV + B_WIDTH, d_)),
            pl.BlockSpec((tm, LANES), lambda b, i: (i, 0)),
            pl.BlockSpec((tm, LANES), lambda b, i: (i, 0)),
            full((1, A_Q)), full((1, A_KV)), full((A_Q, A_Q)), full((A_KV, A_KV)),
        ],
        out_specs=[
            pl.BlockSpec((1, A_HEADS, tm, LANES), lambda b, i: (b, 0, i, 0)),
            pl.BlockSpec((1, tm, LANES), lambda b, i: (b, i, 0)),
            pl.BlockSpec((1, A_KV_HEADS, HEAD_DIM + ONES_ROWS, tm), lambda b, i: (b, 0, 0, i)),
            pl.BlockSpec((1, B_HEADS, 2, tm, LANES), lambda b, i: (b, 0, 0, i, 0)),
            pl.BlockSpec((1, B_HEADS, tm, LANES), lambda b, i: (b, 0, i, 0)),
            pl.BlockSpec((1, B_HEADS, B_VDIM + ONES_ROWS, tm), lambda b, i: (b, 0, 0, i)),
        ],
        out_shape=[
            jax.ShapeDtypeStruct((b_, A_HEADS, s_, LANES), BF16),
            jax.ShapeDtypeStruct((b_, s_, LANES), BF16),
            jax.ShapeDtypeStruct((b_, A_KV_HEADS, HEAD_DIM + ONES_ROWS, s_), BF16),
            jax.ShapeDtypeStruct((b_, B_HEADS, 2, s_, LANES), BF16),
            jax.ShapeDtypeStruct((b_, B_HEADS, s_, LANES), BF16),
            jax.ShapeDtypeStruct((b_, B_HEADS, B_VDIM + ONES_ROWS, s_), BF16),
        ],
        compiler_params=_cparams(("parallel", "parallel")),
        name="inproj",
    )(x, w_in_bf, wv_t, cs, sn, gq, gk, gmq, gmk)


def _softmax_step(s, smax, vt_chunk, m_ref, acc_ref):
    m_prev = m_ref[...]
    m_cur = jnp.maximum(m_prev, smax)
    alpha = jnp.exp2(m_prev - m_cur)
    p = jnp.exp2(s - m_cur).astype(BF16)
    acc_ref[...] = alpha * acc_ref[...] + _dot(vt_chunk, p)
    m_ref[...] = m_cur


def _init_stats(m_ref, acc_ref):
    m_ref[...] = jnp.full(m_ref.shape, NEG_BIG, F32)
    acc_ref[...] = jnp.zeros(acc_ref.shape, F32)


def _pipelined_chunks(nk, scores, consume, s_ref, smax_ref):
    assert nk % 2 == 0
    group = CHUNKS_PER_TRIP if nk % CHUNKS_PER_TRIP == 0 else 2

    def produce(j, slot):
        s = scores(j)
        s_ref[slot] = s
        smax_ref[slot] = jnp.max(s, axis=0, keepdims=True)

    def step(j, slot, prefetch):
        if prefetch:
            produce(j + 1, 1 - slot)
        consume(j, s_ref[slot], smax_ref[slot])

    def trip(i, carry):
        for u in range(group):
            step(group * i + u, u % 2, True)
        return carry

    produce(0, 0)
    lax.fori_loop(0, nk // group - 1, trip, 0)
    for u in range(group):
        step(nk - group + u, u % 2, u < group - 1)


def _attn_a_kernel(q_ref, k_ref, vt_ref, o_ref, m_ref, acc_ref, s_ref, smax_ref, *, tk):
    tq = q_ref.shape[2]
    cols = A_REP * tq
    nk = k_ref.shape[1] // tk
    q = q_ref[0].reshape(cols, LANES)
    _init_stats(m_ref, acc_ref)

    def scores(j):
        return _dot_nt(k_ref[0, pl.ds(pl.multiple_of(j * tk, tk), tk), :], q)

    def consume(j, s, smax):
        vt = vt_ref[0, 0, :, pl.ds(pl.multiple_of(j * tk, tk), tk)]
        _softmax_step(s, smax, vt, m_ref, acc_ref)

    _pipelined_chunks(nk, scores, consume, s_ref, smax_ref)
    o = acc_ref[0:HEAD_DIM, :] / acc_ref[HEAD_DIM:HEAD_DIM + 1, :]
    o = jnp.concatenate([o[:, r * tq:(r + 1) * tq] for r in range(A_REP)], axis=0)
    o_ref[0] = o.T.astype(BF16)


def _attn_a(qa, ka, vat):
    b_, _, s_, _ = qa.shape
    tq = min(TQ_A, s_)
    tk = min(TK_A, s_)
    cols = A_REP * tq
    vrows = vat.shape[2]
    return pl.pallas_call(
        functools.partial(_attn_a_kernel, tk=tk),
        grid=(b_, A_KV_HEADS, s_ // tq),
        in_specs=[
            pl.BlockSpec((1, A_REP, tq, LANES), lambda b, g, i: (b, g, i, 0)),
            pl.BlockSpec((1, s_, LANES), lambda b, g, i: (b, 0, 0)),
            pl.BlockSpec((1, 1, vrows, s_), lambda b, g, i: (b, g, 0, 0)),
        ],
        out_specs=pl.BlockSpec((1, tq, A_REP * HEAD_DIM), lambda b, g, i: (b, i, g)),
        out_shape=jax.ShapeDtypeStruct((b_, s_, A_Q), BF16),
        scratch_shapes=[pltpu.VMEM((1, cols), F32), pltpu.VMEM((vrows, cols), F32),
                        pltpu.VMEM((2, tk, cols), F32), pltpu.VMEM((2, 1, cols), F32)],
        compiler_params=_cparams(("parallel", "parallel", "arbitrary")),
        name="attn_a",
    )(qa, ka, vat)


def _attn_b_kernel(thr_ref, val_ref, q_ref, k_ref, vt_ref, lq1_ref, lk1_ref, lq2_ref, lk2_ref, sub_ref,
                   o_ref, bias_ref, m_ref, acc_ref, s_ref, smax_ref, *, lam_init):
    h = pl.program_id(1)
    qi = pl.program_id(2)
    t = q_ref.shape[3]
    nk = k_ref.shape[2] // t
    half = NUM_BUCKETS // 2
    reach = (BIAS_TILES - 1) // 2

    @pl.when(qi == 0)
    def _build_bias():
        key = lax.broadcasted_iota(I32, (t, t), 0)
        qry = lax.broadcasted_iota(I32, (t, t), 1)
        for d in range(-reach, reach + 1):
            rel = key - qry + d * t
            n = jnp.abs(rel)
            neg = jnp.full((t, t), val_ref[h, 0], F32)
            pos = jnp.full((t, t), val_ref[h, half], F32)
            for j in range(1, half):
                ge = n >= thr_ref[j]
                neg = jnp.where(ge, val_ref[h, j], neg)
                pos = jnp.where(ge, val_ref[h, half + j], pos)
            bias_ref[d + reach] = jnp.where(rel > 0, pos, neg)

    q = q_ref[0, 0].reshape(2 * t, LANES)
    _init_stats(m_ref, acc_ref)

    def scores(j):
        bias = bias_ref[jnp.clip(j - qi, -reach, reach) + reach]
        kc = k_ref[0, 0, pl.ds(pl.multiple_of(j * t, t), t), :]
        return _dot_nt(kc, q) + jnp.concatenate([bias, bias], axis=1)

    def consume(j, s, smax):
        vt = vt_ref[0, 0, :, pl.ds(pl.multiple_of(j * t, t), t)]
        _softmax_step(s, smax, vt, m_ref, acc_ref)

    _pipelined_chunks(nk, scores, consume, s_ref, smax_ref)

    lam =(jnp.exp(jnp.sum(lq1_ref[...] * lk1_ref[...], axis=1, keepdims=True))
           - jnp.exp(jnp.sum(lq2_ref[...] * lk2_ref[...], axis=1, keepdims=True)) + lam_init)
    o = acc_ref[0:B_VDIM, :] / acc_ref[B_VDIM:B_VDIM + 1, :]
    o = o[:, 0:t] - lam * o[:, t:2 * t]
    ms = jnp.mean(o * o, axis=0, keepdims=True)
    o = o * lax.rsqrt(ms + LN_EPS) * sub_ref[...] * (1.0 - lam_init)
    o_ref[0] = o.T.astype(BF16)


def _attn_b(qb, kb, vbt, thr, vals, lq1, lk1, lq2, lk2, subln_col, lam_init):
    b_, _, _, s_, _ = qb.shape
    t = min(T_B, s_)
    assert t >= MAX_DISTANCE
    vrows = vbt.shape[2]
    vec = lambda n: pl.BlockSpec((1, n), lambda b, h, i, *_: (0, 0))
    grid_spec = pltpu.PrefetchScalarGridSpec(
        num_scalar_prefetch=2,
        grid=(b_, B_HEADS, s_ // t),
        in_specs=[
            pl.BlockSpec((1, 1, 2, t, LANES), lambda b, h, i, *_: (b, h, 0, i, 0)),
            pl.BlockSpec((1, 1, s_, LANES), lambda b, h, i, *_: (b, h, 0, 0)),
            pl.BlockSpec((1, 1, vrows, s_), lambda b, h, i, *_: (b, h, 0, 0)),
            vec(HEAD_DIM), vec(HEAD_DIM), vec(HEAD_DIM), vec(HEAD_DIM),
            pl.BlockSpec((B_VDIM, 1), lambda b, h, i, *_: (0, 0)),
        ],
        out_specs=pl.BlockSpec((1, t, B_VDIM), lambda b, h, i, *_: (b, i, h)),
        scratch_shapes=[pltpu.VMEM((BIAS_TILES, t, t), F32), pltpu.VMEM((1, 2 * t), F32),
                        pltpu.VMEM((vrows, 2 * t), F32),
                        pltpu.VMEM((2, t, 2 * t), F32), pltpu.VMEM((2, 1, 2 * t), F32)],
    )
    return pl.pallas_call(
        functools.partial(_attn_b_kernel, lam_init=lam_init),
        grid_spec=grid_spec,
        out_shape=jax.ShapeDtypeStruct((b_, s_, B_WIDTH), BF16),
        compiler_params=_cparams(("parallel", "parallel", "arbitrary")),
        name="attn_b",
    )(thr, vals, qb, kb, vbt, lq1, lk1, lq2, lk2, subln_col)


def _outproj_kernel(oa_ref, ob_ref, x_ref, wa_ref, wb_ref, g_ref, b_ref, rh_ref, rl_ref,
                    x1_ref, x1b_ref, aff_ref, *, alpha):
    tm = x_ref.shape[1]
    mix = _dot(oa_ref[0], wa_ref[...]) + _dot(ob_ref[0], wb_ref[...])
    x1 = _layer_norm(alpha * x_ref[0] + mix, g_ref[...], b_ref[...])
    x1_ref[0] = x1
    x1b_ref[0] = x1.astype(BF16)
    hi, lo = _split_bf16(x1)
    logits = _dot(hi, rh_ref[...]) + _dot(lo, rh_ref[...]) + _dot(hi, rl_ref[...])
    lane = lax.broadcasted_iota(I32, (tm, LANES), 1)
    logits = jnp.where(lane < N_EXPERTS, logits, NEG_BIG)
    e = jnp.exp(logits - jnp.max(logits, axis=1, keepdims=True))
    aff = e / jnp.sum(e, axis=1, keepdims=True)
    aff_t = aff.T
    for c in range(tm // LANES):
        aff_ref[0, c] = aff_t[0:N_EXPERTS, c * LANES:(c + 1) * LANES]


def _outproj(oa, ob, x, wo_a, wo_b, g, b, rh, rl, alpha):
    b_, s_, d_ = x.shape
    tm = min(TM_PROJ, s_)
    nch = tm // LANES
    full = lambda shape: pl.BlockSpec(shape, lambda b, i: (0,) * len(shape))
    return pl.pallas_call(
        functools.partial(_outproj_kernel, alpha=alpha),
        grid=(b_, s_ // tm),
        in_specs=[
            pl.BlockSpec((1, tm, A_Q), lambda b, i: (b, i, 0)),
            pl.BlockSpec((1, tm, B_WIDTH), lambda b, i: (b, i, 0)),
            pl.BlockSpec((1, tm, d_), lambda b, i: (b, i, 0)),
            full((A_Q, d_)), full((B_WIDTH, d_)), full((1, d_)), full((1, d_)),
            full((d_, LANES)), full((d_, LANES)),
        ],
        out_specs=[
            pl.BlockSpec((1, tm, d_), lambda b, i: (b, i, 0)),
            pl.BlockSpec((1, tm, d_), lambda b, i: (b, i, 0)),
            pl.BlockSpec((1, nch, N_EXPERTS, LANES), lambda b, i: (b, i, 0, 0)),
        ],
        out_shape=[
            jax.ShapeDtypeStruct((b_, s_, d_), F32),
            jax.ShapeDtypeStruct((b_, s_, d_), BF16),
            jax.ShapeDtypeStruct((b_, s_ // LANES, N_EXPERTS, LANES), F32),
        ],
        compiler_params=_cparams(("parallel", "parallel")),
        name="outproj",
    )(oa, ob, x, wo_a, wo_b, g, b, rh, rl)


def _route_kernel(aff_ref, tri_ref, pos_ref, gate_ref, off_ref, *, cap):
    nc = aff_ref.shape[1]
    aff = aff_ref[0]
    bits = pltpu.bitcast(aff, I32)

    def count(mask):
        per_lane = jnp.sum(mask.astype(F32), axis=0)
        return jnp.sum(per_lane, axis=1, keepdims=True)

    thr = jnp.zeros((N_EXPERTS, 1), I32)
    for bit in range(30, -1, -1):
        cand = thr | (1 << bit)
        thr = jnp.where(count(bits >= cand[None]) >= cap, cand, thr)

    gt = bits > thr[None]
    eq = bits == thr[None]
    need = cap - count(gt)

    def prefix(mask, out_ref, extra_ref):
        mb = mask.astype(BF16).reshape(nc * N_EXPERTS, LANES)
        incl = _dot(mb, tri_ref[...]).reshape(nc, N_EXPERTS, LANES)
        excl = incl - mask.astype(F32)
        tot = incl[:, :, LANES - 1:LANES]

        def body(c, off):
            out_ref[0, c] = (excl[c] + off).astype(I32)
            if extra_ref is not None:
                extra_ref[0, c] = jnp.broadcast_to(off, (N_EXPERTS, LANES)).astype(I32)
            return off + tot[c]

        off = jnp.zeros((N_EXPERTS, 1), F32)
        for c in range(nc):
            off = body(c, off)

    prefix(eq, pos_ref, None)
    sel = gt | (eq & (pos_ref[0] < need[None].astype(I32)))
    prefix(sel, pos_ref, off_ref)
    pos_ref[0] = jnp.where(sel, pos_ref[0], -1)
    gate_ref[0] = jnp.where(sel, aff, 0.0)


def _route(aff, tri, cap):
    b_, nc, _, _ = aff.shape
    blk = pl.BlockSpec((1, nc, N_EXPERTS, LANES), lambda b: (b, 0, 0, 0))
    return pl.pallas_call(
        functools.partial(_route_kernel, cap=cap),
        grid=(b_,),
        in_specs=[blk, pl.BlockSpec((LANES, LANES), lambda b: (0, 0))],
        out_specs=[blk, blk, blk],
        out_shape=[jax.ShapeDtypeStruct(aff.shape, I32), jax.ShapeDtypeStruct(aff.shape, F32),
                   jax.ShapeDtypeStruct(aff.shape, I32)],
        compiler_params=_cparams(("parallel",)),
        name="route",
    )(aff, tri)


def _slab_geometry(starts_ref, b, j, e):
    start = starts_ref[b, j, e]
    count = starts_ref[b, j + 1, e] - start
    base = (start // BF16_ROWS) * BF16_ROWS
    nslab = (start - base + count + SLAB - 1) // SLAB
    return base, jnp.where(count > 0, nslab, 0)


def _dispatch_kernel(starts_ref, x_ref, pos_ref, xe_ref):
    b = pl.program_id(0)
    eg = pl.program_id(1)
    j = pl.program_id(2)
    t = x_ref.shape[1]

    @pl.when(j == 0)
    def _zero():
        xe_ref[...] = jnp.zeros(xe_ref.shape, BF16)

    xt = x_ref[0]
    rid = lax.broadcasted_iota(I32, (SLAB, t), 0)
    geo, local = [], []
    for el in range(E_GROUP):
        e = eg * E_GROUP + el
        geo.append(_slab_geometry(starts_ref, b, j, e))
        prow = jnp.concatenate(
            [pos_ref[0, c, pl.ds(e, 1), :] for c in range(t // LANES)], axis=1)
        local.append(jnp.broadcast_to(prow - geo[el][0], (SLAB, t)))

    def add_rows(el, k, rows):
        dst = pl.ds(pl.multiple_of(geo[el][0] + k * SLAB, BF16_ROWS), SLAB)
        xe_ref[0, el, dst, :] = xe_ref[0, el, dst, :] + rows.astype(BF16)

    onehot = jnp.concatenate([(loc == rid).astype(BF16) for loc in local], axis=0)
    rows = _dot(onehot, xt)
    for el in range(E_GROUP):
        add_rows(el, 0, rows[el * SLAB:(el + 1) * SLAB])

    for el in range(E_GROUP):
        def body(k, carry, el=el):
            add_rows(el, k, _dot((local[el] == rid + k * SLAB).astype(BF16), xt))
            return carry

        lax.fori_loop(1, geo[el][1], body, 0)


def _dispatch(starts, x1b, pos, capp):
    b_, s_, d_ = x1b.shape
    t = min(T_TOK, s_)
    grid_spec = pltpu.PrefetchScalarGridSpec(
        num_scalar_prefetch=1,
        grid=(b_, N_EXPERTS // E_GROUP, s_ // t),
        in_specs=[
            pl.BlockSpec((1, t, d_), lambda b, g, j, *_: (b, j, 0)),
            pl.BlockSpec((1, t // LANES, N_EXPERTS, LANES), lambda b, g, j, *_: (b, j, 0, 0)),
        ],
        out_specs=pl.BlockSpec((1, E_GROUP, capp, d_), lambda b, g, j, *_: (b, g, 0, 0)),
    )
    return pl.pallas_call(
        _dispatch_kernel,
        grid_spec=grid_spec,
        out_shape=jax.ShapeDtypeStruct((b_, N_EXPERTS, capp, d_), BF16),
        compiler_params=_cparams(("parallel", "parallel", "arbitrary")),
        name="dispatch",
    )(starts, x1b, pos)


def _ffn_kernel(xe_ref, wg_ref, wu_ref, wd_ref, y_ref, wg_s, wu_s, wd_s, acc_ref, *, cap):
    fc = pl.program_id(1)
    b = pl.program_id(2)
    nf = pl.num_programs(1)

    @pl.when(b == 0)
    def _cast_weights():
        wg_s[...] = wg_ref[0].astype(BF16)
        wu_s[...] = wu_ref[0].astype(BF16)
        wd_s[...] = wd_ref[0].astype(BF16)

    xe = xe_ref[0, 0]
    fcw = wg_s.shape[1]
    sub = min(FC_SUB, fcw)

    def gate_up(n):
        cols = slice(n * sub, (n + 1) * sub)
        hg = _dot(xe, wg_s[:, cols])
        hu = _dot(xe, wu_s[:, cols])
        return (hg * jax.nn.sigmoid(hg) * hu).astype(BF16)

    nsub = fcw // sub
    acts = [gate_up(0)]
    part = None
    for n in range(nsub):
        if n + 1 < nsub:
            acts.append(gate_up(n + 1))
        d = _dot(acts[n], wd_s[n * sub:(n + 1) * sub, :])
        part = d if part is None else part + d

    @pl.when(fc == 0)
    def _first():
        acc_ref[b] = part

    @pl.when((fc > 0) & (fc < nf - 1))
    def _middle():
        acc_ref[b] = acc_ref[b] + part

    @pl.when(fc == nf - 1)
    def _emit():
        y_ref[0, 0, 0:cap, :] = (acc_ref[b] + part).astype(BF16)
        y_ref[0, 0, cap:, :] = jnp.zeros((y_ref.shape[2] - cap, y_ref.shape[3]), BF16)


def _ffn(xe, w_gate, w_up, w_down, cap):
    b_, ne, capp, d_ = xe.shape
    f_ = w_gate.shape[2]
    fcw = min(FC, f_)
    nf = f_ // fcw
    assert nf >= 2
    return pl.pallas_call(
        functools.partial(_ffn_kernel, cap=cap),
        grid=(ne, nf, b_),
        in_specs=[
            pl.BlockSpec((1, 1, cap, d_), lambda e, f, b: (b, e, 0, 0)),
            pl.BlockSpec((1, d_, fcw), lambda e, f, b: (e, 0, f)),
            pl.BlockSpec((1, d_, fcw), lambda e, f, b: (e, 0, f)),
            pl.BlockSpec((1, fcw, d_), lambda e, f, b: (e, f, 0)),
        ],
        out_specs=pl.BlockSpec((1, 1, capp, d_), lambda e, f, b: (jnp.where(f == nf - 1, b, 0), e, 0, 0)),
        out_shape=jax.ShapeDtypeStruct((b_, ne, capp, d_), BF16),
        scratch_shapes=[pltpu.VMEM((d_, fcw), BF16), pltpu.VMEM((d_, fcw), BF16),
                        pltpu.VMEM((fcw, d_), BF16), pltpu.VMEM((b_, cap, d_), F32)],
        compiler_params=_cparams(("arbitrary", "arbitrary", "arbitrary")),
        name="ffn",
    )(xe, w_gate, w_up, w_down)


def _combine_kernel(starts_ref, y_hbm, pos_ref, gate_ref, x1_ref, p_ref, ex_ref, wpg_ref, wpp_ref,
                    g2_ref, b2_ref, g3_ref, b3_ref, o_ref, ybuf, xbuf, sem, xsem, moe_ref, *, alpha):
    b = pl.program_id(0)
    j = pl.program_id(1)
    nt = pl.num_programs(1)
    t = x1_ref.shape[1]
    slot = j % 2

    def first_slab_copy(jj, e, sl):
        base, _ = _slab_geometry(starts_ref, b, jj, e)
        src = y_hbm.at[b, e, pl.ds(pl.multiple_of(base, BF16_ROWS), SLAB), :]
        return pltpu.make_async_copy(src, ybuf.at[sl, pl.ds(e * SLAB, SLAB), :], sem.at[sl, e])

    @pl.when(j == 0)
    def _prime():
        for e in range(N_EXPERTS):
            first_slab_copy(j, e, slot).start()

    @pl.when(j + 1 < nt)
    def _prefetch_next_tile():
        for e in range(N_EXPERTS):
            first_slab_copy(j + 1, e, 1 - slot).start()

    pad = jnp.zeros((LANES - N_EXPERTS, LANES), F32)
    pos_t = jnp.concatenate(
        [jnp.concatenate([pos_ref[0, c].astype(F32), pad], axis=0).T for c in range(t // LANES)], axis=0)
    gate_t = jnp.concatenate(
        [jnp.concatenate([gate_ref[0, c], pad], axis=0).T for c in range(t // LANES)], axis=0)

    geo = [_slab_geometry(starts_ref, b, j, e) for e in range(N_EXPERTS)]
    lane_e = lax.broadcasted_iota(I32, (1, LANES), 1)
    slab_e = lax.broadcasted_iota(I32, (1, N_EXPERTS * SLAB), 1) // SLAB
    start_row = jnp.zeros((1, LANES), F32)
    shift_row = jnp.zeros((1, N_EXPERTS * SLAB), F32)
    for e, (base, _) in enumerate(geo):
        start = starts_ref[b, j, e]
        start_row = jnp.where(lane_e == e, start.astype(F32), start_row)
        shift_row = jnp.where(slab_e == e, (start - base).astype(F32), shift_row)
    rank = jnp.where(pos_t >= 0.0, pos_t - start_row, -512.0).astype(BF16)
    rank_x = _dot(rank, ex_ref[...])
    row_in_slab = (lax.broadcasted_iota(I32, (1, N_EXPERTS * SLAB), 1) % SLAB).astype(F32)
    match = rank_x == row_in_slab - shift_row
    g_hi, g_lo = _split_bf16(gate_t)
    oh_hi = jnp.where(match, _dot(g_hi, ex_ref[...]), 0.0).astype(BF16)
    oh_lo = jnp.where(match, _dot(g_lo, ex_ref[...]), 0.0).astype(BF16)

    for e in range(N_EXPERTS):
        first_slab_copy(j, e, slot).wait()
    ycat = ybuf[slot]
    moe_ref[...] = _dot(oh_hi, ycat) + _dot(oh_lo, ycat)

    cid = lax.broadcasted_iota(I32, (t, SLAB), 1).astype(F32)
    for e, (base, nslab) in enumerate(geo):
        def body(k, carry, e=e, base=base):
            src = y_hbm.at[b, e, pl.ds(pl.multiple_of(base + k * SLAB, BF16_ROWS), SLAB), :]
            cp = pltpu.make_async_copy(src, xbuf, xsem)
            cp.start()
            local = jnp.broadcast_to(pos_t[:, e:e + 1] - (base + k * SLAB).astype(F32), (t, SLAB))
            onehot = (local == cid).astype(BF16)
            cp.wait()
            moe_ref[...] = moe_ref[...] + gate_t[:, e:e + 1] * _dot(onehot, xbuf[...])
            return carry

        lax.fori_loop(1, nslab, body, 0)

    x2 = _layer_norm(alpha * x1_ref[0] + moe_ref[...], g2_ref[...], b2_ref[...])
    gate = jax.nn.sigmoid(_dot(x2.astype(BF16), wpg_ref[...]))
    ple = _dot(p_ref[0].astype(BF16), wpp_ref[...]) * gate
    o_ref[0] = _layer_norm(alpha * x2 + ple, g3_ref[...], b3_ref[...])


def _combine(starts, y, pos, gate, x1, p, wpg, wpp, g2, b2, g3, b3, alpha):
    b_, s_, d_ = x1.shape
    t = min(T_TOK, s_)
    pd = p.shape[2]
    ex = (jnp.arange(LANES)[:, None] == jnp.arange(N_EXPERTS * SLAB)[None, :] // SLAB).astype(BF16)
    full = lambda shape: pl.BlockSpec(shape, lambda b, j, *_: (0,) * len(shape))
    tile4 = pl.BlockSpec((1, t // LANES, N_EXPERTS, LANES), lambda b, j, *_: (b, j, 0, 0))
    grid_spec = pltpu.PrefetchScalarGridSpec(
        num_scalar_prefetch=1,
        grid=(b_, s_ // t),
        in_specs=[
            pl.BlockSpec(memory_space=pl.ANY),
            tile4, tile4,
            pl.BlockSpec((1, t, d_), lambda b, j, *_: (b, j, 0)),
            pl.BlockSpec((1, t, pd), lambda b, j, *_: (b, j, 0)),
            full((LANES, N_EXPERTS * SLAB)),
            full((d_, d_)), full((pd, d_)),
            full((1, d_)), full((1, d_)), full((1, d_)), full((1, d_)),
        ],
        out_specs=pl.BlockSpec((1, t, d_), lambda b, j, *_: (b, j, 0)),
        scratch_shapes=[pltpu.VMEM((2, N_EXPERTS * SLAB, d_), BF16),
                        pltpu.VMEM((SLAB, d_), BF16),
                        pltpu.SemaphoreType.DMA((2, N_EXPERTS)),
                        pltpu.SemaphoreType.DMA(()),
                        pltpu.VMEM((t, d_), F32)],
    )
    return pl.pallas_call(
        functools.partial(_combine_kernel, alpha=alpha),
        grid_spec=grid_spec,
        out_shape=jax.ShapeDtypeStruct((b_, s_, d_), F32),
        compiler_params=_cparams(("parallel", "arbitrary")),
        name="combine",
    )(starts, y, pos, gate, x1, p, ex, wpg, wpp, g2, b2, g3, b3)


def _rope_tables(s_):
    half = HEAD_DIM // 2
    inv = ROPE_THETA ** (-jnp.arange(0, half, 2, dtype=F32) / half)
    t = jnp.arange(s_, dtype=jnp.int32)
    row = (t // GRID_W).astype(F32)[:, None] * inv[None, :]
    col = (t % GRID_W).astype(F32)[:, None] * inv[None, :]
    cos = jnp.concatenate([jnp.cos(row), jnp.cos(row), jnp.cos(col), jnp.cos(col)], axis=1)
    sin = jnp.concatenate([-jnp.sin(row), jnp.sin(row), -jnp.sin(col), jnp.sin(col)], axis=1)
    reps = LANES // HEAD_DIM
    return jnp.tile(cos, (1, reps)), jnp.tile(sin, (1, reps))


def _t5_bucket(rel):
    half = NUM_BUCKETS // 2
    max_exact = half // 2
    ret = (rel > 0).astype(jnp.int32) * half
    n = jnp.abs(rel)
    nf = jnp.maximum(n, 1).astype(F32)
    large = max_exact + (jnp.log(nf / max_exact) / math.log(MAX_DISTANCE / max_exact)
                         * (half - max_exact)).astype(jnp.int32)
    large = jnp.minimum(large, half - 1)
    return ret + jnp.where(n < max_exact, n, large)


def _bucket_thresholds():
    half = NUM_BUCKETS // 2
    n = jnp.arange(0, MAX_DISTANCE + 1, dtype=jnp.int32)
    bk = _t5_bucket(-n)
    j = jnp.arange(half, dtype=jnp.int32)
    return jnp.sum((bk[None, :] < j[:, None]).astype(jnp.int32), axis=1)


def _group_mean_matrix(width):
    g = jnp.arange(width, dtype=jnp.int32) // HEAD_DIM
    return ((g[:, None] == g[None, :]).astype(F32) / HEAD_DIM).astype(BF16)


def kernel(x, p, w_in, w_out, a_q_norm, a_k_norm, b_lambda_q1, b_lambda_k1, b_lambda_q2, b_lambda_k2,
           b_subln, rel_bias, ln1_g, ln1_b, w_router, w_gate, w_up, w_down, ln2_g, ln2_b,
           w_ple_gate, w_ple_proj, ln3_g, ln3_b):
    b_, s_, d_ = x.shape
    depth = w_in.shape[0]
    alpha = (2 * depth) ** 0.25
    cap = EC_CAPACITY_FACTOR * s_ // N_EXPERTS
    t_tok = min(T_TOK, s_)
    capp = cap + t_tok + BF16_ROWS
    assert s_ % GRID_W == 0 and s_ % LANES == 0 and cap % BF16_ROWS == 0

    cs, sn = _rope_tables(s_)
    thr = _bucket_thresholds()
    bias_vals = (rel_bias.astype(F32) * LOG2E).T
    gmq, gmk = _group_mean_matrix(A_Q), _group_mean_matrix(A_KV)
    tri = (jnp.arange(LANES)[:, None] <= jnp.arange(LANES)[None, :]).astype(BF16)
    row = lambda v: v.astype(F32).reshape(1, -1)

    for i in range(depth):
        lam_init = 0.8 - 0.6 * math.exp(-0.3 * i)
        qa, ka, vat, qb, kb, vbt = _inproj(
            x, w_in[i].astype(BF16), cs, sn,
            jnp.tile(row(a_q_norm[i]), (1, A_HEADS)), jnp.tile(row(a_k_norm[i]), (1, A_KV_HEADS)), gmq, gmk)
        oa = _attn_a(qa, ka, vat)
        ob = _attn_b(qb, kb, vbt, thr, bias_vals, row(b_lambda_q1[i]), row(b_lambda_k1[i]),
                     row(b_lambda_q2[i]), row(b_lambda_k2[i]), b_subln[i].astype(F32).reshape(-1, 1), lam_init)
        wr = jnp.pad(w_router[i].astype(F32), ((0, 0), (0, LANES - N_EXPERTS)))
        rh = wr.astype(BF16)
        rl = (wr - rh.astype(F32)).astype(BF16)
        wo = w_out[i].astype(BF16)
        x1, x1b, aff = _outproj(oa, ob, x, wo[:A_Q], wo[A_Q:], row(ln1_g[i]), row(ln1_b[i]), rh, rl, alpha)
        pos, gate, off = _route(aff, tri, cap)
        starts = jnp.concatenate(
            [off[:, ::t_tok // LANES, :, 0], jnp.full((b_, 1, N_EXPERTS), cap, jnp.int32)], axis=1)
        xe = _dispatch(starts, x1b, pos, capp)
        y = _ffn(xe, w_gate[i], w_up[i], w_down[i], cap)
        x = _combine(starts, y, pos, gate, x1, p[i], w_ple_gate[i].astype(BF16), w_ple_proj[i].astype(BF16),
                     row(ln2_g[i]), row(ln2_b[i]), row(ln3_g[i]), row(ln3_b[i]), alpha)
    return x
```

```python
import functools
import math

import jax
import jax.numpy as jnp
from jax import lax
from jax.experimental import pallas as pl
from jax.experimental.pallas import tpu as pltpu

F32 = jnp.float32
BF16 = jnp.bfloat16
I32 = jnp.int32

HEAD_DIM = 64
A_HEADS = 8
A_KV_HEADS = 2
A_REP = A_HEADS // A_KV_HEADS
B_HEADS = 4
B_VDIM = 2 * HEAD_DIM
A_Q = A_HEADS * HEAD_DIM
A_KV = A_KV_HEADS * HEAD_DIM
B_QK = B_HEADS * 2 * HEAD_DIM
B_WIDTH = B_HEADS * B_VDIM
ATTN_SCALE = HEAD_DIM ** -0.5
GRID_W = 64
ROPE_THETA = 10000.0
NUM_BUCKETS = 32
MAX_DISTANCE = 128
N_EXPERTS = 16
EC_CAPACITY_FACTOR = 2
LN_EPS = 1e-5
QK_EPS = 1e-6
LOG2E = math.log2(math.e)

LANES = 128
BF16_ROWS = 16
ONES_ROWS = BF16_ROWS
VMEM_LIMIT = 56 * 1024 * 1024

TM_PROJ = 512
TQ_A = 256
TK_A = 512
T_B = 512
CHUNKS_PER_TRIP = 4
BIAS_TILES = 5
T_TOK = 256
SLAB = 64
E_GROUP = 8
FC = 512
FC_SUB = 256
NEG_BIG = -1e30


def _cparams(sem):
    return pltpu.CompilerParams(dimension_semantics=sem, vmem_limit_bytes=VMEM_LIMIT)


def _dot(a, b):
    return jnp.dot(a, b, preferred_element_type=F32)


def _dot_nt(a, b):
    return lax.dot_general(a, b, (((1,), (1,)), ((), ())), preferred_element_type=F32)


def _layer_norm(y, g, b):
    mu = jnp.mean(y, axis=-1, keepdims=True)
    yc = y - mu
    var = jnp.mean(yc * yc, axis=-1, keepdims=True)
    return yc * lax.rsqrt(var + LN_EPS) * g + b


def _split_bf16(v):
    hi = v.astype(BF16)
    lo = (v - hi.astype(F32)).astype(BF16)
    return hi, lo


def _inproj_kernel(x_ref, w_ref, wvt_ref, cs_ref, sn_ref, gq_ref, gk_ref, gmq_ref, gmk_ref,
                   qa_ref, ka_ref, vat_ref, qb_ref, kb_ref, vbt_ref):
    tm = x_ref.shape[1]
    xb = x_ref[0].astype(BF16)
    lane = lax.broadcasted_iota(I32, (tm, LANES), 1)
    lo_half = lane < HEAD_DIM
    cs = cs_ref[...]
    sn = sn_ref[...]
    qscale = ATTN_SCALE * LOG2E
    ones_rows = (lax.broadcasted_iota(I32, (ONES_ROWS, tm), 0) == 0).astype(BF16)

    def group_rms(v, gm_ref, gain):
        hi, lo = _split_bf16(v * v)
        ms = _dot(hi, gm_ref[...]) + _dot(lo, gm_ref[...])
        return v * lax.rsqrt(ms + QK_EPS) * gain

    def rope(v, reps):
        width = v.shape[1]
        lane_w = lax.broadcasted_iota(I32, v.shape, 1)
        first = (lane_w % (HEAD_DIM // 2)) < (HEAD_DIM // 4)
        rot = jnp.where(first, pltpu.roll(v, width - HEAD_DIM // 4, 1), pltpu.roll(v, HEAD_DIM // 4, 1))
        c = jnp.concatenate([cs] * reps, axis=1) if reps > 1 else cs
        s = jnp.concatenate([sn] * reps, axis=1) if reps > 1 else sn
        return v * c + rot * s

    qa = _dot(xb, w_ref[:, 0:A_Q])
    qa = rope(group_rms(qa, gmq_ref, gq_ref[...]), A_Q // LANES) * qscale
    for c in range(A_Q // LANES):
        chunk = qa[:, c * LANES:(c + 1) * LANES]
        swapped = pltpu.roll(chunk, HEAD_DIM, 1)
        if (2 * c) // A_REP == 0:
            even, odd = jnp.where(lo_half, chunk, 0.0), jnp.where(lo_half, swapped, 0.0)
        else:
            even, odd = jnp.where(lo_half, 0.0, swapped), jnp.where(lo_half, 0.0, chunk)
        qa_ref[0, 2 * c] = even.astype(BF16)
        qa_ref[0, 2 * c + 1] = odd.astype(BF16)

    ka = _dot(xb, w_ref[:, A_Q:A_Q + A_KV])
    ka_ref[0] = rope(group_rms(ka, gmk_ref, gk_ref[...]), 1).astype(BF16)
    v_t = _dot_nt(wvt_ref[...], xb).astype(BF16)
    for g in range(A_KV_HEADS):
        vat_ref[0, g] = jnp.concatenate([v_t[g * HEAD_DIM:(g + 1) * HEAD_DIM], ones_rows], axis=0)

    o = A_Q + 2 * A_KV
    qb = _dot(xb, w_ref[:, o:o + B_QK]) * qscale
    kb = _dot(xb, w_ref[:, o + B_QK:o + 2 * B_QK])
    for h in range(B_HEADS):
        chunk = qb[:, h * LANES:(h + 1) * LANES]
        qb_ref[0, h, 0] = jnp.where(lo_half, chunk, 0.0).astype(BF16)
        qb_ref[0, h, 1] = jnp.where(lo_half, 0.0, chunk).astype(BF16)
        kb_ref[0, h] = kb[:, h * LANES:(h + 1) * LANES].astype(BF16)
        vbt_ref[0, h] = jnp.concatenate(
            [v_t[A_KV + h * B_VDIM:A_KV + (h + 1) * B_VDIM], ones_rows], axis=0)


def _inproj(x, w_in_bf, cs, sn, gq, gk, gmq, gmk):
    b_, s_, d_ = x.shape
    tm = min(TM_PROJ, s_)
    nst = s_ // tm
    in_w = w_in_bf.shape[1]
    vb0 = A_Q + 2 * A_KV + 2 * B_QK
    wv_t = jnp.concatenate([w_in_bf[:, A_Q + A_KV:A_Q + 2 * A_KV], w_in_bf[:, vb0:vb0 + B_WIDTH]], axis=1).T
    full = lambda shape: pl.BlockSpec(shape, lambda b, i: (0,) * len(shape))
    return pl.pallas_call(
        _inproj_kernel,
        grid=(b_, nst),
        in_specs=[
            pl.BlockSpec((1, tm, d_), lambda b, i: (b, i, 0)),
            full((d_, in_w)),
            full((A_KV + B_WIDTH, d_)),
            pl.BlockSpec((tm, LANES), lambda b, i: (i, 0)),
            pl.BlockSpec((tm, LANES), lambda b, i: (i, 0)),
            full((1, A_Q)), full((1, A_KV)), full((A_Q, A_Q)), full((A_KV, A_KV)),
        ],
        out_specs=[
            pl.BlockSpec((1, A_HEADS, tm, LANES), lambda b, i: (b, 0, i, 0)),
            pl.BlockSpec((1, tm, LANES), lambda b, i: (b, i, 0)),
            pl.BlockSpec((1, A_KV_HEADS, HEAD_DIM + ONES_ROWS, tm), lambda b, i: (b, 0, 0, i)),
            pl.BlockSpec((1, B_HEADS, 2, tm, LANES), lambda b, i: (b, 0, 0, i, 0)),
            pl.BlockSpec((1, B_HEADS, tm, LANES), lambda b, i: (b, 0, i, 0)),
            pl.BlockSpec((1, B_HEADS, B_VDIM + ONES_ROWS, tm), lambda b, i: (b, 0, 0, i)),
        ],
        out_shape=[
            jax.ShapeDtypeStruct((b_, A_HEADS, s_, LANES), BF16),
            jax.ShapeDtypeStruct((b_, s_, LANES), BF16),
            jax.ShapeDtypeStruct((b_, A_KV_HEADS, HEAD_DIM + ONES_ROWS, s_), BF16),
            jax.ShapeDtypeStruct((b_, B_HEADS, 2, s_, LANES), BF16),
            jax.ShapeDtypeStruct((b_, B_HEADS, s_, LANES), BF16),
            jax.ShapeDtypeStruct((b_, B_HEADS, B_VDIM + ONES_ROWS, s_), BF16),
        ],
        compiler_params=_cparams(("parallel", "parallel")),
        name="inproj",
    )(x, w_in_bf, wv_t, cs, sn, gq, gk, gmq, gmk)


def _softmax_step(s, smax, vt_chunk, shift, m_ref, acc_ref):
    m_prev = m_ref[...]
    m_cur = jnp.maximum(m_prev, smax)
    alpha = jnp.exp2(m_prev - m_cur)
    p = jnp.exp2(s - (m_cur - shift)).astype(BF16)
    acc_ref[...] = alpha * acc_ref[...] + _dot(vt_chunk, p)
    m_ref[...] = m_cur


def _init_stats(m_ref, acc_ref):
    m_ref[...] = jnp.full(m_ref.shape, NEG_BIG, F32)
    acc_ref[...] = jnp.zeros(acc_ref.shape, F32)


def _pipelined_chunks(nk, scores, consume, s_ref, smax_ref):
    assert nk % 2 == 0
    group = CHUNKS_PER_TRIP if nk % CHUNKS_PER_TRIP == 0 else 2

    def produce(j, slot):
        s, shift = scores(j)
        s_ref[slot] = s
        smax_ref[slot] = jnp.max(s, axis=0, keepdims=True) + shift

    def step(j, slot, prefetch):
        if prefetch:
            produce(j + 1, 1 - slot)
        consume(j, s_ref[slot], smax_ref[slot])

    def trip(i, carry):
        for u in range(group):
            step(group * i + u, u % 2, True)
        return carry

    produce(0, 0)
    lax.fori_loop(0, nk // group - 1, trip, 0)
    for u in range(group):
        step(nk - group + u, u % 2, u < group - 1)


def _attn_a_kernel(q_ref, k_ref, vt_ref, o_ref, m_ref, acc_ref, s_ref, smax_ref, *, tk):
    tq = q_ref.shape[2]
    cols = A_REP * tq
    nk = k_ref.shape[1] // tk
    q = q_ref[0].reshape(cols, LANES)
    _init_stats(m_ref, acc_ref)

    def scores(j):
        return _dot_nt(k_ref[0, pl.ds(pl.multiple_of(j * tk, tk), tk), :], q), 0.0

    def consume(j, s, smax):
        vt = vt_ref[0, 0, :, pl.ds(pl.multiple_of(j * tk, tk), tk)]
        _softmax_step(s, smax, vt, 0.0, m_ref, acc_ref)

    _pipelined_chunks(nk, scores, consume, s_ref, smax_ref)
    o = acc_ref[0:HEAD_DIM, :] / acc_ref[HEAD_DIM:HEAD_DIM + 1, :]
    o = jnp.concatenate([o[:, r * tq:(r + 1) * tq] for r in range(A_REP)], axis=0)
    o_ref[0] = o.T.astype(BF16)


def _attn_a(qa, ka, vat):
    b_, _, s_, _ = qa.shape
    tq = min(TQ_A, s_)
    tk = min(TK_A, s_)
    cols = A_REP * tq
    vrows = vat.shape[2]
    return pl.pallas_call(
        functools.partial(_attn_a_kernel, tk=tk),
        grid=(b_, A_KV_HEADS, s_ // tq),
        in_specs=[
            pl.BlockSpec((1, A_REP, tq, LANES), lambda b, g, i: (b, g, i, 0)),
            pl.BlockSpec((1, s_, LANES), lambda b, g, i: (b, 0, 0)),
            pl.BlockSpec((1, 1, vrows, s_), lambda b, g, i: (b, g, 0, 0)),
        ],
        out_specs=pl.BlockSpec((1, tq, A_REP * HEAD_DIM), lambda b, g, i: (b, i, g)),
        out_shape=jax.ShapeDtypeStruct((b_, s_, A_Q), BF16),
        scratch_shapes=[pltpu.VMEM((1, cols), F32), pltpu.VMEM((vrows, cols), F32),
                        pltpu.VMEM((2, tk, cols), F32), pltpu.VMEM((2, 1, cols), F32)],
        compiler_params=_cparams(("parallel", "parallel", "arbitrary")),
        name="attn_a",
    )(qa, ka, vat)


def _attn_b_kernel(thr_ref, val_ref, q_ref, k_ref, vt_ref, lq1_ref, lk1_ref, lq2_ref, lk2_ref, sub_ref,
                   o_ref, bias_ref, m_ref, acc_ref, s_ref, smax_ref, *, lam_init):
    h = pl.program_id(1)
    qi = pl.program_id(2)
    t = q_ref.shape[3]
    nk = k_ref.shape[2] // t
    half = NUM_BUCKETS // 2
    reach = (BIAS_TILES - 1) // 2

    @pl.when(qi == 0)
    def _build_bias():
        key = lax.broadcasted_iota(I32, (t, t), 0)
        qry = lax.broadcasted_iota(I32, (t, t), 1)
        for d in range(-reach, reach + 1):
            rel = key - qry + d * t
            n = jnp.abs(rel)
            neg = jnp.full((t, t), val_ref[h, 0], F32)
            pos = jnp.full((t, t), val_ref[h, half], F32)
            for j in range(1, half):
                ge = n >= thr_ref[j]
                neg = jnp.where(ge, val_ref[h, j], neg)
                pos = jnp.where(ge, val_ref[h, half + j], pos)
            bias_ref[d + reach] = jnp.where(rel > 0, pos, neg)

    q = q_ref[0, 0].reshape(2 * t, LANES)
    _init_stats(m_ref, acc_ref)

    n_near = min(nk, 2 * reach - 1)
    near0 = jnp.clip(qi - (n_near // 2), 0, nk - n_near)
    assert nk <= CHUNKS_PER_TRIP or (nk % CHUNKS_PER_TRIP == 0 and CHUNKS_PER_TRIP > n_near)

    def chunk_of(v):
        if isinstance(v, int) and v >= nk - n_near:
            return near0 + (v - (nk - n_near)), True
        return v + jnp.where(v >= near0, n_near, 0), False

    def far_shift(j):
        return jnp.where(j < qi, val_ref[h, half - 1], val_ref[h, NUM_BUCKETS - 1])

    def scores(v):
        j, near = chunk_of(v)
        s = _dot_nt(k_ref[0, 0, pl.ds(pl.multiple_of(j * t, t), t), :], q)
        if near:
            bias = bias_ref[jnp.clip(j - qi, -reach, reach) + reach]
            return s + jnp.concatenate([bias, bias], axis=1), 0.0
        return s, far_shift(j)

    def consume(v, s, smax):
        j, near = chunk_of(v)
        vt = vt_ref[0, 0, :, pl.ds(pl.multiple_of(j * t, t), t)]
        _softmax_step(s, smax, vt, 0.0 if near else far_shift(j), m_ref, acc_ref)

    _pipelined_chunks(nk, scores, consume, s_ref, smax_ref)

    lam =(jnp.exp(jnp.sum(lq1_ref[...] * lk1_ref[...], axis=1, keepdims=True))
           - jnp.exp(jnp.sum(lq2_ref[...] * lk2_ref[...], axis=1, keepdims=True)) + lam_init)
    o = acc_ref[0:B_VDIM, :] / acc_ref[B_VDIM:B_VDIM + 1, :]
    o = o[:, 0:t] - lam * o[:, t:2 * t]
    ms = jnp.mean(o * o, axis=0, keepdims=True)
    o = o * lax.rsqrt(ms + LN_EPS) * sub_ref[...] * (1.0 - lam_init)
    o_ref[0] = o.T.astype(BF16)


def _attn_b(qb, kb, vbt, thr, vals, lq1, lk1, lq2, lk2, subln_col, lam_init):
    b_, _, _, s_, _ = qb.shape
    t = min(T_B, s_)
    assert t >= MAX_DISTANCE
    vrows = vbt.shape[2]
    vec = lambda n: pl.BlockSpec((1, n), lambda b, h, i, *_: (0, 0))
    grid_spec = pltpu.PrefetchScalarGridSpec(
        num_scalar_prefetch=2,
        grid=(b_, B_HEADS, s_ // t),
        in_specs=[
            pl.BlockSpec((1, 1, 2, t, LANES), lambda b, h, i, *_: (b, h, 0, i, 0)),
            pl.BlockSpec((1, 1, s_, LANES), lambda b, h, i, *_: (b, h, 0, 0)),
            pl.BlockSpec((1, 1, vrows, s_), lambda b, h, i, *_: (b, h, 0, 0)),
            vec(HEAD_DIM), vec(HEAD_DIM), vec(HEAD_DIM), vec(HEAD_DIM),
            pl.BlockSpec((B_VDIM, 1), lambda b, h, i, *_: (0, 0)),
        ],
        out_specs=pl.BlockSpec((1, t, B_VDIM), lambda b, h, i, *_: (b, i, h)),
        scratch_shapes=[pltpu.VMEM((BIAS_TILES, t, t), F32), pltpu.VMEM((1, 2 * t), F32),
                        pltpu.VMEM((vrows, 2 * t), F32),
                        pltpu.VMEM((2, t, 2 * t), F32), pltpu.VMEM((2, 1, 2 * t), F32)],
    )
    return pl.pallas_call(
        functools.partial(_attn_b_kernel, lam_init=lam_init),
        grid_spec=grid_spec,
        out_shape=jax.ShapeDtypeStruct((b_, s_, B_WIDTH), BF16),
        compiler_params=_cparams(("parallel", "parallel", "arbitrary")),
        name="attn_b",
    )(thr, vals, qb, kb, vbt, lq1, lk1, lq2, lk2, subln_col)


def _outproj_kernel(oa_ref, ob_ref, x_ref, wa_ref, wb_ref, g_ref, b_ref, r_ref,
                    x1_ref, x1b_ref, aff_ref, *, alpha):
    tm = x_ref.shape[1]
    mix = _dot(oa_ref[0], wa_ref[...]) + _dot(ob_ref[0], wb_ref[...])
    x1 = _layer_norm(alpha * x_ref[0] + mix, g_ref[...], b_ref[...])
    x1_ref[0] = x1
    x1b_ref[0] = x1.astype(BF16)
    hi, lo = _split_bf16(x1)
    parts = _dot(hi, r_ref[...]) + _dot(lo, r_ref[...])
    logits = parts[:, 0:LANES] + parts[:, LANES:2 * LANES]
    lane = lax.broadcasted_iota(I32, (tm, LANES), 1)
    logits = jnp.where(lane < N_EXPERTS, logits, NEG_BIG)
    e = jnp.exp(logits - jnp.max(logits, axis=1, keepdims=True))
    aff = e / jnp.sum(e, axis=1, keepdims=True)
    aff_t = aff.T
    for c in range(tm // LANES):
        aff_ref[0, c] = aff_t[0:N_EXPERTS, c * LANES:(c + 1) * LANES]


def _outproj(oa, ob, x, wo_a, wo_b, g, b, r_hi_lo, alpha):
    b_, s_, d_ = x.shape
    tm = min(TM_PROJ, s_)
    nch = tm // LANES
    full = lambda shape: pl.BlockSpec(shape, lambda b, i: (0,) * len(shape))
    return pl.pallas_call(
        functools.partial(_outproj_kernel, alpha=alpha),
        grid=(b_, s_ // tm),
        in_specs=[
            pl.BlockSpec((1, tm, A_Q), lambda b, i: (b, i, 0)),
            pl.BlockSpec((1, tm, B_WIDTH), lambda b, i: (b, i, 0)),
            pl.BlockSpec((1, tm, d_), lambda b, i: (b, i, 0)),
            full((A_Q, d_)), full((B_WIDTH, d_)), full((1, d_)), full((1, d_)),
            full((d_, 2 * LANES)),
        ],
        out_specs=[
            pl.BlockSpec((1, tm, d_), lambda b, i: (b, i, 0)),
            pl.BlockSpec((1, tm, d_), lambda b, i: (b, i, 0)),
            pl.BlockSpec((1, nch, N_EXPERTS, LANES), lambda b, i: (b, i, 0, 0)),
        ],
        out_shape=[
            jax.ShapeDtypeStruct((b_, s_, d_), F32),
            jax.ShapeDtypeStruct((b_, s_, d_), BF16),
            jax.ShapeDtypeStruct((b_, s_ // LANES, N_EXPERTS, LANES), F32),
        ],
        compiler_params=_cparams(("parallel", "parallel")),
        name="outproj",
    )(oa, ob, x, wo_a, wo_b, g, b, r_hi_lo)


def _route_kernel(aff_ref, tri_ref, pos_ref, gate_ref, off_ref, *, cap):
    nc = aff_ref.shape[1]
    aff = aff_ref[0]
    bits = pltpu.bitcast(aff, I32)

    def count(mask):
        per_lane = jnp.sum(mask.astype(F32), axis=0)
        return jnp.sum(per_lane, axis=1, keepdims=True)

    thr = jnp.zeros((N_EXPERTS, 1), I32)
    for bit in range(30, -1, -1):
        cand = thr | (1 << bit)
        thr = jnp.where(count(bits >= cand[None]) >= cap, cand, thr)

    gt = bits > thr[None]
    eq = bits == thr[None]
    need = cap - count(gt)

    def prefix(mask, out_ref, extra_ref):
        mb = mask.astype(BF16).reshape(nc * N_EXPERTS, LANES)
        incl = _dot(mb, tri_ref[...]).reshape(nc, N_EXPERTS, LANES)
        excl = incl - mask.astype(F32)
        tot = incl[:, :, LANES - 1:LANES]

        def body(c, off):
            out_ref[0, c] = (excl[c] + off).astype(I32)
            if extra_ref is not None:
                extra_ref[0, c] = jnp.broadcast_to(off, (N_EXPERTS, LANES)).astype(I32)
            return off + tot[c]

        off = jnp.zeros((N_EXPERTS, 1), F32)
        for c in range(nc):
            off = body(c, off)

    prefix(eq, pos_ref, None)
    sel = gt | (eq & (pos_ref[0] < need[None].astype(I32)))
    prefix(sel, pos_ref, off_ref)
    pos_ref[0] = jnp.where(sel, pos_ref[0], -1)
    gate_ref[0] = jnp.where(sel, aff, 0.0)


def _route(aff, tri, cap):
    b_, nc, _, _ = aff.shape
    blk = pl.BlockSpec((1, nc, N_EXPERTS, LANES), lambda b: (b, 0, 0, 0))
    return pl.pallas_call(
        functools.partial(_route_kernel, cap=cap),
        grid=(b_,),
        in_specs=[blk, pl.BlockSpec((LANES, LANES), lambda b: (0, 0))],
        out_specs=[blk, blk, blk],
        out_shape=[jax.ShapeDtypeStruct(aff.shape, I32), jax.ShapeDtypeStruct(aff.shape, F32),
                   jax.ShapeDtypeStruct(aff.shape, I32)],
        compiler_params=_cparams(("parallel",)),
        name="route",
    )(aff, tri)


def _slab_geometry(starts_ref, b, j, e):
    start = starts_ref[b, j, e]
    count = starts_ref[b, j + 1, e] - start
    base = (start // BF16_ROWS) * BF16_ROWS
    nslab = (start - base + count + SLAB - 1) // SLAB
    return base, jnp.where(count > 0, nslab, 0)


def _dispatch_kernel(starts_ref, x_ref, pos_ref, xe_ref):
    b = pl.program_id(0)
    eg = pl.program_id(1)
    j = pl.program_id(2)
    t = x_ref.shape[1]

    @pl.when(j == 0)
    def _zero():
        xe_ref[...] = jnp.zeros(xe_ref.shape, BF16)

    xt = x_ref[0]
    rid = lax.broadcasted_iota(I32, (SLAB, t), 0)
    geo, local = [], []
    for el in range(E_GROUP):
        e = eg * E_GROUP + el
        geo.append(_slab_geometry(starts_ref, b, j, e))
        prow = jnp.concatenate(
            [pos_ref[0, c, pl.ds(e, 1), :] for c in range(t // LANES)], axis=1)
        local.append(jnp.broadcast_to(prow - geo[el][0], (SLAB, t)))

    def add_rows(el, k, rows):
        dst = pl.ds(pl.multiple_of(geo[el][0] + k * SLAB, BF16_ROWS), SLAB)
        xe_ref[0, el, dst, :] = xe_ref[0, el, dst, :] + rows.astype(BF16)

    onehot = jnp.concatenate([(loc == rid).astype(BF16) for loc in local], axis=0)
    rows = _dot(onehot, xt)
    for el in range(E_GROUP):
        add_rows(el, 0, rows[el * SLAB:(el + 1) * SLAB])

    for el in range(E_GROUP):
        def body(k, carry, el=el):
            add_rows(el, k, _dot((local[el] == rid + k * SLAB).astype(BF16), xt))
            return carry

        lax.fori_loop(1, geo[el][1], body, 0)


def _dispatch(starts, x1b, pos, capp):
    b_, s_, d_ = x1b.shape
    t = min(T_TOK, s_)
    grid_spec = pltpu.PrefetchScalarGridSpec(
        num_scalar_prefetch=1,
        grid=(b_, N_EXPERTS // E_GROUP, s_ // t),
        in_specs=[
            pl.BlockSpec((1, t, d_), lambda b, g, j, *_: (b, j, 0)),
            pl.BlockSpec((1, t // LANES, N_EXPERTS, LANES), lambda b, g, j, *_: (b, j, 0, 0)),
        ],
        out_specs=pl.BlockSpec((1, E_GROUP, capp, d_), lambda b, g, j, *_: (b, g, 0, 0)),
    )
    return pl.pallas_call(
        _dispatch_kernel,
        grid_spec=grid_spec,
        out_shape=jax.ShapeDtypeStruct((b_, N_EXPERTS, capp, d_), BF16),
        compiler_params=_cparams(("parallel", "parallel", "arbitrary")),
        name="dispatch",
    )(starts, x1b, pos)


def _ffn_kernel(xe_ref, wg_ref, wu_ref, wd_ref, y_ref, wg_s, wu_s, wd_s, acc_ref, *, cap):
    fc = pl.program_id(1)
    b = pl.program_id(2)
    nf = pl.num_programs(1)

    @pl.when(b == 0)
    def _cast_weights():
        wg_s[...] = wg_ref[0].astype(BF16)
        wu_s[...] = wu_ref[0].astype(BF16)
        wd_s[...] = wd_ref[0].astype(BF16)

    xe = xe_ref[0, 0]
    fcw = wg_s.shape[1]
    sub = min(FC_SUB, fcw)

    def gate_up(n):
        cols = slice(n * sub, (n + 1) * sub)
        hg = _dot(xe, wg_s[:, cols])
        hu = _dot(xe, wu_s[:, cols])
        return (hg * jax.nn.sigmoid(hg) * hu).astype(BF16)

    nsub = fcw // sub
    acts = [gate_up(0)]
    part = None
    for n in range(nsub):
        if n + 1 < nsub:
            acts.append(gate_up(n + 1))
        d = _dot(acts[n], wd_s[n * sub:(n + 1) * sub, :])
        part = d if part is None else part + d

    @pl.when(fc == 0)
    def _first():
        acc_ref[b] = part

    @pl.when((fc > 0) & (fc < nf - 1))
    def _middle():
        acc_ref[b] = acc_ref[b] + part

    @pl.when(fc == nf - 1)
    def _emit():
        y_ref[0, 0, 0:cap, :] = (acc_ref[b] + part).astype(BF16)
        y_ref[0, 0, cap:, :] = jnp.zeros((y_ref.shape[2] - cap, y_ref.shape[3]), BF16)


def _ffn(xe, w_gate, w_up, w_down, cap):
    b_, ne, capp, d_ = xe.shape
    f_ = w_gate.shape[2]
    fcw = min(FC, f_)
    nf = f_ // fcw
    assert nf >= 2
    return pl.pallas_call(
        functools.partial(_ffn_kernel, cap=cap),
        grid=(ne, nf, b_),
        in_specs=[
            pl.BlockSpec((1, 1, cap, d_), lambda e, f, b: (b, e, 0, 0)),
            pl.BlockSpec((1, d_, fcw), lambda e, f, b: (e, 0, f)),
            pl.BlockSpec((1, d_, fcw), lambda e, f, b: (e, 0, f)),
            pl.BlockSpec((1, fcw, d_), lambda e, f, b: (e, f, 0)),
        ],
        out_specs=pl.BlockSpec((1, 1, capp, d_), lambda e, f, b: (jnp.where(f == nf - 1, b, 0), e, 0, 0)),
        out_shape=jax.ShapeDtypeStruct((b_, ne, capp, d_), BF16),
        scratch_shapes=[pltpu.VMEM((d_, fcw), BF16), pltpu.VMEM((d_, fcw), BF16),
                        pltpu.VMEM((fcw, d_), BF16), pltpu.VMEM((b_, cap, d_), F32)],
        compiler_params=_cparams(("arbitrary", "arbitrary", "arbitrary")),
        name="ffn",
    )(xe, w_gate, w_up, w_down)


def _combine_kernel(starts_ref, y_hbm, pos_ref, gate_ref, x1_ref, p_ref, ex_ref, wpg_ref, wpp_ref,
                    g2_ref, b2_ref, g3_ref, b3_ref, o_ref, ybuf, xbuf, sem, xsem, moe_ref, *, alpha):
    b = pl.program_id(0)
    j = pl.program_id(1)
    nt = pl.num_programs(1)
    t = x1_ref.shape[1]
    slot = j % 2

    def first_slab_copy(jj, e, sl):
        base, _ = _slab_geometry(starts_ref, b, jj, e)
        src = y_hbm.at[b, e, pl.ds(pl.multiple_of(base, BF16_ROWS), SLAB), :]
        return pltpu.make_async_copy(src, ybuf.at[sl, pl.ds(e * SLAB, SLAB), :], sem.at[sl, e])

    @pl.when(j == 0)
    def _prime():
        for e in range(N_EXPERTS):
            first_slab_copy(j, e, slot).start()

    @pl.when(j + 1 < nt)
    def _prefetch_next_tile():
        for e in range(N_EXPERTS):
            first_slab_copy(j + 1, e, 1 - slot).start()

    pad = jnp.zeros((LANES - N_EXPERTS, LANES), F32)
    pos_t = jnp.concatenate(
        [jnp.concatenate([pos_ref[0, c].astype(F32), pad], axis=0).T for c in range(t // LANES)], axis=0)
    gate_t = jnp.concatenate(
        [jnp.concatenate([gate_ref[0, c], pad], axis=0).T for c in range(t // LANES)], axis=0)

    geo = [_slab_geometry(starts_ref, b, j, e) for e in range(N_EXPERTS)]
    lane_e = lax.broadcasted_iota(I32, (1, LANES), 1)
    slab_e = lax.broadcasted_iota(I32, (1, N_EXPERTS * SLAB), 1) // SLAB
    start_row = jnp.zeros((1, LANES), F32)
    shift_row = jnp.zeros((1, N_EXPERTS * SLAB), F32)
    for e, (base, _) in enumerate(geo):
        start = starts_ref[b, j, e]
        start_row = jnp.where(lane_e == e, start.astype(F32), start_row)
        shift_row = jnp.where(slab_e == e, (start - base).astype(F32), shift_row)
    rank = jnp.where(pos_t >= 0.0, pos_t - start_row, -512.0).astype(BF16)
    rank_x = _dot(rank, ex_ref[...])
    row_in_slab = (lax.broadcasted_iota(I32, (1, N_EXPERTS * SLAB), 1) % SLAB).astype(F32)
    match = rank_x == row_in_slab - shift_row
    g_hi, g_lo = _split_bf16(gate_t)
    oh_hi = jnp.where(match, _dot(g_hi, ex_ref[...]), 0.0).astype(BF16)
    oh_lo = jnp.where(match, _dot(g_lo, ex_ref[...]), 0.0).astype(BF16)

    for e in range(N_EXPERTS):
        first_slab_copy(j, e, slot).wait()
    ycat = ybuf[slot]
    moe_ref[...] = _dot(oh_hi, ycat) + _dot(oh_lo, ycat)

    cid = lax.broadcasted_iota(I32, (t, SLAB), 1).astype(F32)
    for e, (base, nslab) in enumerate(geo):
        def body(k, carry, e=e, base=base):
            src = y_hbm.at[b, e, pl.ds(pl.multiple_of(base + k * SLAB, BF16_ROWS), SLAB), :]
            cp = pltpu.make_async_copy(src, xbuf, xsem)
            cp.start()
            local = jnp.broadcast_to(pos_t[:, e:e + 1] - (base + k * SLAB).astype(F32), (t, SLAB))
            onehot = (local == cid).astype(BF16)
            cp.wait()
            moe_ref[...] = moe_ref[...] + gate_t[:, e:e + 1] * _dot(onehot, xbuf[...])
            return carry

        lax.fori_loop(1, nslab, body, 0)

    x2 = _layer_norm(alpha * x1_ref[0] + moe_ref[...], g2_ref[...], b2_ref[...])
    gate = jax.nn.sigmoid(_dot(x2.astype(BF16), wpg_ref[...]))
    ple = _dot(p_ref[0].astype(BF16), wpp_ref[...]) * gate
    o_ref[0] = _layer_norm(alpha * x2 + ple, g3_ref[...], b3_ref[...])


def _combine(starts, y, pos, gate, x1, p, wpg, wpp, g2, b2, g3, b3, alpha):
    b_, s_, d_ = x1.shape
    t = min(T_TOK, s_)
    pd = p.shape[2]
    ex = (jnp.arange(LANES)[:, None] == jnp.arange(N_EXPERTS * SLAB)[None, :] // SLAB).astype(BF16)
    full = lambda shape: pl.BlockSpec(shape, lambda b, j, *_: (0,) * len(shape))
    tile4 = pl.BlockSpec((1, t // LANES, N_EXPERTS, LANES), lambda b, j, *_: (b, j, 0, 0))
    grid_spec = pltpu.PrefetchScalarGridSpec(
        num_scalar_prefetch=1,
        grid=(b_, s_ // t),
        in_specs=[
            pl.BlockSpec(memory_space=pl.ANY),
            tile4, tile4,
            pl.BlockSpec((1, t, d_), lambda b, j, *_: (b, j, 0)),
            pl.BlockSpec((1, t, pd), lambda b, j, *_: (b, j, 0)),
            full((LANES, N_EXPERTS * SLAB)),
            full((d_, d_)), full((pd, d_)),
            full((1, d_)), full((1, d_)), full((1, d_)), full((1, d_)),
        ],
        out_specs=pl.BlockSpec((1, t, d_), lambda b, j, *_: (b, j, 0)),
        scratch_shapes=[pltpu.VMEM((2, N_EXPERTS * SLAB, d_), BF16),
                        pltpu.VMEM((SLAB, d_), BF16),
                        pltpu.SemaphoreType.DMA((2, N_EXPERTS)),
                        pltpu.SemaphoreType.DMA(()),
                        pltpu.VMEM((t, d_), F32)],
    )
    return pl.pallas_call(
        functools.partial(_combine_kernel, alpha=alpha),
        grid_spec=grid_spec,
        out_shape=jax.ShapeDtypeStruct((b_, s_, d_), F32),
        compiler_params=_cparams(("parallel", "arbitrary")),
        name="combine",
    )(starts, y, pos, gate, x1, p, ex, wpg, wpp, g2, b2, g3, b3)


def _rope_tables(s_):
    half = HEAD_DIM // 2
    inv = ROPE_THETA ** (-jnp.arange(0, half, 2, dtype=F32) / half)
    t = jnp.arange(s_, dtype=jnp.int32)
    row = (t // GRID_W).astype(F32)[:, None] * inv[None, :]
    col = (t % GRID_W).astype(F32)[:, None] * inv[None, :]
    cos = jnp.concatenate([jnp.cos(row), jnp.cos(row), jnp.cos(col), jnp.cos(col)], axis=1)
    sin = jnp.concatenate([-jnp.sin(row), jnp.sin(row), -jnp.sin(col), jnp.sin(col)], axis=1)
    reps = LANES // HEAD_DIM
    return jnp.tile(cos, (1, reps)), jnp.tile(sin, (1, reps))


def _t5_bucket(rel):
    half = NUM_BUCKETS // 2
    max_exact = half // 2
    ret = (rel > 0).astype(jnp.int32) * half
    n = jnp.abs(rel)
    nf = jnp.maximum(n, 1).astype(F32)
    large = max_exact + (jnp.log(nf / max_exact) / math.log(MAX_DISTANCE / max_exact)
                         * (half - max_exact)).astype(jnp.int32)
    large = jnp.minimum(large, half - 1)
    return ret + jnp.where(n < max_exact, n, large)


def _bucket_thresholds():
    half = NUM_BUCKETS // 2
    n = jnp.arange(0, MAX_DISTANCE + 1, dtype=jnp.int32)
    bk = _t5_bucket(-n)
    j = jnp.arange(half, dtype=jnp.int32)
    return jnp.sum((bk[None, :] < j[:, None]).astype(jnp.int32), axis=1)


def _group_mean_matrix(width):
    g = jnp.arange(width, dtype=jnp.int32) // HEAD_DIM
    return ((g[:, None] == g[None, :]).astype(F32) / HEAD_DIM).astype(BF16)


def kernel(x, p, w_in, w_out, a_q_norm, a_k_norm, b_lambda_q1, b_lambda_k1, b_lambda_q2, b_lambda_k2,
           b_subln, rel_bias, ln1_g, ln1_b, w_router, w_gate, w_up, w_down, ln2_g, ln2_b,
           w_ple_gate, w_ple_proj, ln3_g, ln3_b):
    b_, s_, d_ = x.shape
    depth = w_in.shape[0]
    alpha = (2 * depth) ** 0.25
    cap = EC_CAPACITY_FACTOR * s_ // N_EXPERTS
    t_tok = min(T_TOK, s_)
    capp = cap + SLAB
    assert s_ % GRID_W == 0 and s_ % LANES == 0 and cap % BF16_ROWS == 0

    cs, sn = _rope_tables(s_)
    thr = _bucket_thresholds()
    bias_vals = (rel_bias.astype(F32) * LOG2E).T
    gmq, gmk = _group_mean_matrix(A_Q), _group_mean_matrix(A_KV)
    tri = (jnp.arange(LANES)[:, None] <= jnp.arange(LANES)[None, :]).astype(BF16)
    row = lambda v: v.astype(F32).reshape(1, -1)

    for i in range(depth):
        lam_init = 0.8 - 0.6 * math.exp(-0.3 * i)
        qa, ka, vat, qb, kb, vbt = _inproj(
            x, w_in[i].astype(BF16), cs, sn,
            jnp.tile(row(a_q_norm[i]), (1, A_HEADS)), jnp.tile(row(a_k_norm[i]), (1, A_KV_HEADS)), gmq, gmk)
        oa = _attn_a(qa, ka, vat)
        ob = _attn_b(qb, kb, vbt, thr, bias_vals, row(b_lambda_q1[i]), row(b_lambda_k1[i]),
                     row(b_lambda_q2[i]), row(b_lambda_k2[i]), b_subln[i].astype(F32).reshape(-1, 1), lam_init)
        wr = jnp.pad(w_router[i].astype(F32), ((0, 0), (0, LANES - N_EXPERTS)))
        rh = wr.astype(BF16)
        rl = (wr - rh.astype(F32)).astype(BF16)
        wo = w_out[i].astype(BF16)
        x1, x1b, aff = _outproj(oa, ob, x, wo[:A_Q], wo[A_Q:], row(ln1_g[i]), row(ln1_b[i]),
                                jnp.concatenate([rh, rl], axis=1), alpha)
        pos, gate, off = _route(aff, tri, cap)
        starts = jnp.concatenate(
            [off[:, ::t_tok // LANES, :, 0], jnp.full((b_, 1, N_EXPERTS), cap, jnp.int32)], axis=1)
        xe = _dispatch(starts, x1b, pos, capp)
        y = _ffn(xe, w_gate[i], w_up[i], w_down[i], cap)
        x = _combine(starts, y, pos, gate, x1, p[i], w_ple_gate[i].astype(BF16), w_ple_proj[i].astype(BF16),
                     row(ln2_g[i]), row(ln2_b[i]), row(ln3_g[i]), row(ln3_b[i]), alpha)
    return x
```

```python
import functools
import math

import jax
import jax.numpy as jnp
from jax import lax
from jax.experimental import pallas as pl
from jax.experimental.pallas import tpu as pltpu

F32 = jnp.float32
BF16 = jnp.bfloat16
I32 = jnp.int32

HEAD_DIM = 64
A_HEADS = 8
A_KV_HEADS = 2
A_REP = A_HEADS // A_KV_HEADS
B_HEADS = 4
B_VDIM = 2 * HEAD_DIM
A_Q = A_HEADS * HEAD_DIM
A_KV = A_KV_HEADS * HEAD_DIM
B_QK = B_HEADS * 2 * HEAD_DIM
B_WIDTH = B_HEADS * B_VDIM
ATTN_SCALE = HEAD_DIM ** -0.5
GRID_W = 64
ROPE_THETA = 10000.0
NUM_BUCKETS = 32
MAX_DISTANCE = 128
N_EXPERTS = 16
EC_CAPACITY_FACTOR = 2
LN_EPS = 1e-5
QK_EPS = 1e-6
LOG2E = math.log2(math.e)

LANES = 128
BF16_ROWS = 16
ONES_ROWS = BF16_ROWS
VMEM_LIMIT = 56 * 1024 * 1024

TM_PROJ = 512
TQ_A = 256
TK_A = 512
T_B = 512
SCORE_LOOKAHEAD = 3
MAX_UNROLLED_BLOCKS = 64
B_QUERY_SPLIT = 2
CHUNKS_PER_TRIP = 4
BIAS_TILES = 5
T_TOK = 256
SLAB = 64
E_GROUP = 8
FC = 512
NEG_BIG = -1e30
EXP_RANGE = 40.0


def _cparams(sem):
    return pltpu.CompilerParams(dimension_semantics=sem, vmem_limit_bytes=VMEM_LIMIT)


def _dot(a, b):
    return jnp.dot(a, b, preferred_element_type=F32)


def _dot_nt(a, b):
    return lax.dot_general(a, b, (((1,), (1,)), ((), ())), preferred_element_type=F32)


def _layer_norm(y, g, b):
    mu = jnp.mean(y, axis=-1, keepdims=True)
    yc = y - mu
    var = jnp.mean(yc * yc, axis=-1, keepdims=True)
    return yc * lax.rsqrt(var + LN_EPS) * g + b


def _split_bf16(v):
    hi = v.astype(BF16)
    lo = (v - hi.astype(F32)).astype(BF16)
    return hi, lo


def _inproj_kernel(x_ref, w_ref, wvt_ref, cs_ref, sn_ref, gq_ref, gk_ref, gmq_ref, gmk_ref, gsum_ref,
                   qa_ref, ka_ref, vat_ref, qb_ref, kb_ref, vbt_ref, qn_ref, kn_ref):
    tm = x_ref.shape[1]
    xb = x_ref[0].astype(BF16)
    lane = lax.broadcasted_iota(I32, (tm, LANES), 1)
    lo_half = lane < HEAD_DIM
    cs = cs_ref[...]
    sn = sn_ref[...]
    qscale = ATTN_SCALE * LOG2E
    ones_rows = (lax.broadcasted_iota(I32, (ONES_ROWS, tm), 0) == 0).astype(BF16)

    def group_rms(v, gm_ref, gain):
        hi, lo = _split_bf16(v * v)
        ms = _dot(hi, gm_ref[...]) + _dot(lo, gm_ref[...])
        return v * lax.rsqrt(ms + QK_EPS) * gain

    def rope(v, reps):
        width = v.shape[1]
        lane_w = lax.broadcasted_iota(I32, v.shape, 1)
        first = (lane_w % (HEAD_DIM // 2)) < (HEAD_DIM // 4)
        rot = jnp.where(first, pltpu.roll(v, width - HEAD_DIM // 4, 1), pltpu.roll(v, HEAD_DIM // 4, 1))
        c = jnp.concatenate([cs] * reps, axis=1) if reps > 1 else cs
        s = jnp.concatenate([sn] * reps, axis=1) if reps > 1 else sn
        return v * c + rot * s

    qa = _dot(xb, w_ref[:, 0:A_Q])
    qa = rope(group_rms(qa, gmq_ref, gq_ref[...]), A_Q // LANES) * qscale
    for c in range(A_Q // LANES):
        chunk = qa[:, c * LANES:(c + 1) * LANES]
        swapped = pltpu.roll(chunk, HEAD_DIM, 1)
        if (2 * c) // A_REP == 0:
            even, odd = jnp.where(lo_half, chunk, 0.0), jnp.where(lo_half, swapped, 0.0)
        else:
            even, odd = jnp.where(lo_half, 0.0, swapped), jnp.where(lo_half, 0.0, chunk)
        qa_ref[0, 2 * c] = even.astype(BF16)
        qa_ref[0, 2 * c + 1] = odd.astype(BF16)

    ka = _dot(xb, w_ref[:, A_Q:A_Q + A_KV])
    ka_ref[0] = rope(group_rms(ka, gmk_ref, gk_ref[...]), 1).astype(BF16)
    v_t = _dot_nt(wvt_ref[...], xb).astype(BF16)
    for g in range(A_KV_HEADS):
        vat_ref[0, g] = jnp.concatenate([v_t[g * HEAD_DIM:(g + 1) * HEAD_DIM], ones_rows], axis=0)

    o = A_Q + 2 * A_KV
    qb = _dot(xb, w_ref[:, o:o + B_QK]) * qscale
    kb = _dot(xb, w_ref[:, o + B_QK:o + 2 * B_QK])
    qn_ref[0, 0] = jnp.max(_dot((qb * qb).astype(BF16), gsum_ref[...]), axis=0, keepdims=True)
    kn_ref[0, 0] = jnp.max(_dot((kb * kb).astype(BF16), gsum_ref[...]), axis=0, keepdims=True)
    for h in range(B_HEADS):
        chunk = qb[:, h * LANES:(h + 1) * LANES]
        qb_ref[0, h, 0] = jnp.where(lo_half, chunk, 0.0).astype(BF16)
        qb_ref[0, h, 1] = jnp.where(lo_half, 0.0, chunk).astype(BF16)
        kb_ref[0, h] = kb[:, h * LANES:(h + 1) * LANES].astype(BF16)
        vbt_ref[0, h] = jnp.concatenate(
            [v_t[A_KV + h * B_VDIM:A_KV + (h + 1) * B_VDIM], ones_rows], axis=0)


def _inproj(x, w_in_bf, cs, sn, gq, gk, gmq, gmk):
    b_, s_, d_ = x.shape
    tm = min(TM_PROJ, s_)
    nst = s_ // tm
    in_w = w_in_bf.shape[1]
    vb0 = A_Q + 2 * A_KV + 2 * B_QK
    wv_t = jnp.concatenate([w_in_bf[:, A_Q + A_KV:A_Q + 2 * A_KV], w_in_bf[:, vb0:vb0 + B_WIDTH]], axis=1).T
    gsum = (jnp.arange(B_QK)[:, None] // HEAD_DIM == jnp.arange(LANES)[None, :]).astype(BF16)
    full = lambda shape: pl.BlockSpec(shape, lambda b, i: (0,) * len(shape))
    return pl.pallas_call(
        _inproj_kernel,
        grid=(b_, nst),
        in_specs=[
            pl.BlockSpec((1, tm, d_), lambda b, i: (b, i, 0)),
            full((d_, in_w)),
            full((A_KV + B_WIDTH, d_)),
            pl.BlockSpec((tm, LANES), lambda b, i: (i, 0)),
            pl.BlockSpec((tm, LANES), lambda b, i: (i, 0)),
            full((1, A_Q)), full((1, A_KV)), full((A_Q, A_Q)), full((A_KV, A_KV)), full((B_QK, LANES)),
        ],
        out_specs=[
            pl.BlockSpec((1, A_HEADS, tm, LANES), lambda b, i: (b, 0, i, 0)),
            pl.BlockSpec((1, tm, LANES), lambda b, i: (b, i, 0)),
            pl.BlockSpec((1, A_KV_HEADS, HEAD_DIM + ONES_ROWS, tm), lambda b, i: (b, 0, 0, i)),
            pl.BlockSpec((1, B_HEADS, 2, tm, LANES), lambda b, i: (b, 0, 0, i, 0)),
            pl.BlockSpec((1, B_HEADS, tm, LANES), lambda b, i: (b, 0, i, 0)),
            pl.BlockSpec((1, B_HEADS, B_VDIM + ONES_ROWS, tm), lambda b, i: (b, 0, 0, i)),
            pl.BlockSpec((1, 1, 1, LANES), lambda b, i: (b, i, 0, 0)),
            pl.BlockSpec((1, 1, 1, LANES), lambda b, i: (b, i, 0, 0)),
        ],
        out_shape=[
            jax.ShapeDtypeStruct((b_, A_HEADS, s_, LANES), BF16),
            jax.ShapeDtypeStruct((b_, s_, LANES), BF16),
            jax.ShapeDtypeStruct((b_, A_KV_HEADS, HEAD_DIM + ONES_ROWS, s_), BF16),
            jax.ShapeDtypeStruct((b_, B_HEADS, 2, s_, LANES), BF16),
            jax.ShapeDtypeStruct((b_, B_HEADS, s_, LANES), BF16),
            jax.ShapeDtypeStruct((b_, B_HEADS, B_VDIM + ONES_ROWS, s_), BF16),
            jax.ShapeDtypeStruct((b_, nst, 1, LANES), F32),
            jax.ShapeDtypeStruct((b_, nst, 1, LANES), F32),
        ],
        compiler_params=_cparams(("parallel", "parallel")),
        name="inproj",
    )(x, w_in_bf, wv_t, cs, sn, gq, gk, gmq, gmk, gsum)


def _softmax_step(s, smax, vt_chunk, shift, m_ref, acc_ref):
    m_prev = m_ref[...]
    m_cur = jnp.maximum(m_prev, smax)
    alpha = jnp.exp2(m_prev - m_cur)
    p = jnp.exp2(s - (m_cur - shift)).astype(BF16)
    acc_ref[...] = alpha * acc_ref[...] + _dot(vt_chunk, p)
    m_ref[...] = m_cur


def _init_stats(m_ref, acc_ref):
    m_ref[...] = jnp.full(m_ref.shape, NEG_BIG, F32)
    acc_ref[...] = jnp.zeros(acc_ref.shape, F32)


def _pipelined_chunks(nk, scores, consume, s_ref, smax_ref):
    assert nk % 2 == 0
    group = CHUNKS_PER_TRIP if nk % CHUNKS_PER_TRIP == 0 else 2

    def produce(j, slot):
        s, shift = scores(j)
        s_ref[slot] = s
        smax_ref[slot] = jnp.max(s, axis=0, keepdims=True) + shift

    def step(j, slot, prefetch):
        if prefetch:
            produce(j + 1, 1 - slot)
        consume(j, s_ref[slot], smax_ref[slot])

    def trip(i, carry):
        for u in range(group):
            step(group * i + u, u % 2, True)
        return carry

    produce(0, 0)
    lax.fori_loop(0, nk // group - 1, trip, 0)
    for u in range(group):
        step(nk - group + u, u % 2, u < group - 1)


def _attn_a_kernel(q_ref, k_ref, vt_ref, o_ref, m_ref, acc_ref, s_ref, smax_ref, *, tk):
    tq = q_ref.shape[2]
    cols = A_REP * tq
    nk = k_ref.shape[1] // tk
    q = q_ref[0].reshape(cols, LANES)
    _init_stats(m_ref, acc_ref)

    def scores(j):
        return _dot_nt(k_ref[0, pl.ds(pl.multiple_of(j * tk, tk), tk), :], q), 0.0

    def consume(j, s, smax):
        vt = vt_ref[0, 0, :, pl.ds(pl.multiple_of(j * tk, tk), tk)]
        _softmax_step(s, smax, vt, 0.0, m_ref, acc_ref)

    _pipelined_chunks(nk, scores, consume, s_ref, smax_ref)
    o = acc_ref[0:HEAD_DIM, :] / acc_ref[HEAD_DIM:HEAD_DIM + 1, :]
    o = jnp.concatenate([o[:, r * tq:(r + 1) * tq] for r in range(A_REP)], axis=0)
    o_ref[0] = o.T.astype(BF16)


def _bounded_softmax_groups(ngroups, nk, scores, values_t):
    vdim = values_t.shape[0] - ONES_ROWS
    blocks = [(g, c) for g in range(ngroups) for c in range(nk)]
    assert len(blocks) <= MAX_UNROLLED_BLOCKS
    inflight = [scores(*blocks[i]) for i in range(min(SCORE_LOOKAHEAD, len(blocks)))]
    out = []
    for g in range(ngroups):
        weights = []
        for c in range(nk):
            ahead = g * nk + c + SCORE_LOOKAHEAD
            if ahead < len(blocks):
                inflight.append(scores(*blocks[ahead]))
            weights.append(jnp.exp2(inflight.pop(0)).astype(BF16))
        acc = _dot(values_t, jnp.concatenate(weights, axis=0))
        out.append(acc[0:vdim] / acc[vdim:vdim + 1])
    return out


def _attn_a_bounded_kernel(q_ref, k_ref, vt_ref, o_ref, *, tk):
    nk = k_ref.shape[1] // tk
    qs = [q_ref[0, r] for r in range(A_REP)]

    def scores(r, c):
        return _dot_nt(k_ref[0, c * tk:(c + 1) * tk, :], qs[r])

    outs = _bounded_softmax_groups(A_REP, nk, scores, vt_ref[0, 0])
    o_ref[0] = jnp.concatenate(outs, axis=0).T.astype(BF16)


def _attn_a(qa, ka, vat, bounded):
    b_, _, s_, _ = qa.shape
    tq = min(TQ_A, s_)
    tk = min(TK_A, s_)
    cols = A_REP * tq
    vrows = vat.shape[2]
    if bounded:
        body = functools.partial(_attn_a_bounded_kernel, tk=tk)
        scratch = []
    else:
        body = functools.partial(_attn_a_kernel, tk=tk)
        scratch = [pltpu.VMEM((1, cols), F32), pltpu.VMEM((vrows, cols), F32),
                   pltpu.VMEM((2, tk, cols), F32), pltpu.VMEM((2, 1, cols), F32)]
    return pl.pallas_call(
        body,
        grid=(b_, A_KV_HEADS, s_ // tq),
        in_specs=[
            pl.BlockSpec((1, A_REP, tq, LANES), lambda b, g, i: (b, g, i, 0)),
            pl.BlockSpec((1, s_, LANES), lambda b, g, i: (b, 0, 0)),
            pl.BlockSpec((1, 1, vrows, s_), lambda b, g, i: (b, g, 0, 0)),
        ],
        out_specs=pl.BlockSpec((1, tq, A_REP * HEAD_DIM), lambda b, g, i: (b, i, g)),
        out_shape=jax.ShapeDtypeStruct((b_, s_, A_Q), BF16),
        scratch_shapes=scratch,
        compiler_params=_cparams(("parallel", "parallel", "arbitrary")),
        name="attn_a_bounded" if bounded else "attn_a",
    )(qa, ka, vat)


def _build_bias_tiles(thr_ref, val_ref, bias_ref, h, t):
    half = NUM_BUCKETS // 2
    reach = (BIAS_TILES - 1) // 2
    key = lax.broadcasted_iota(I32, (t, t), 0)
    qry = lax.broadcasted_iota(I32, (t, t), 1)
    for d in range(-reach, reach + 1):
        rel = key - qry + d * t
        n = jnp.abs(rel)
        neg = jnp.full((t, t), val_ref[h, 0], F32)
        pos = jnp.full((t, t), val_ref[h, half], F32)
        for j in range(1, half):
            ge = n >= thr_ref[j]
            neg = jnp.where(ge, val_ref[h, j], neg)
            pos = jnp.where(ge, val_ref[h, half + j], pos)
        bias_ref[d + reach] = jnp.where(rel > 0, pos, neg)


def _chunk_order(qi, nk):
    reach = (BIAS_TILES - 1) // 2
    n_near = min(nk, 2 * reach - 1)
    near0 = jnp.clip(qi - (n_near // 2), 0, nk - n_near)

    def chunk_of(v):
        if isinstance(v, int) and v >= nk - n_near:
            return near0 + (v - (nk - n_near)), True
        return v + jnp.where(v >= near0, n_near, 0), False

    return n_near, chunk_of


def _diff_attn_finish(o, t, lq1_ref, lk1_ref, lq2_ref, lk2_ref, sub_ref, o_ref, lam_init):
    lam = (jnp.exp(jnp.sum(lq1_ref[...] * lk1_ref[...], axis=1, keepdims=True))
           - jnp.exp(jnp.sum(lq2_ref[...] * lk2_ref[...], axis=1, keepdims=True)) + lam_init)
    o = o[:, 0:t] - lam * o[:, t:2 * t]
    ms = jnp.mean(o * o, axis=0, keepdims=True)
    o = o * lax.rsqrt(ms + LN_EPS) * sub_ref[...] * (1.0 - lam_init)
    o_ref[0] = o.T.astype(BF16)


def _attn_b_bounded_kernel(thr_ref, val_ref, q_ref, k_ref, vt_ref, lq1_ref, lk1_ref, lq2_ref, lk2_ref,
                           sub_ref, o_ref, bias_ref, vtp_ref, *, lam_init):
    h = pl.program_id(1)
    qi = pl.program_id(2)
    t = q_ref.shape[3]
    nk = k_ref.shape[2] // t
    half = NUM_BUCKETS // 2
    reach = (BIAS_TILES - 1) // 2
    gw = t // B_QUERY_SPLIT
    ngroups = 2 * B_QUERY_SPLIT

    @pl.when(qi == 0)
    def _bias():
        _build_bias_tiles(thr_ref, val_ref, bias_ref, h, t)

    _, chunk_of = _chunk_order(qi, nk)
    chunks = [chunk_of(v) for v in range(nk)]
    for v, (j, _) in enumerate(chunks):
        vtp_ref[:, v * t:(v + 1) * t] = vt_ref[0, 0, :, pl.ds(pl.multiple_of(j * t, t), t)]
    qs = [q_ref[0, 0, g // B_QUERY_SPLIT, (g % B_QUERY_SPLIT) * gw:(g % B_QUERY_SPLIT + 1) * gw, :]
          for g in range(ngroups)]

    def scores(g, v):
        j, near = chunks[v]
        s = _dot_nt(k_ref[0, 0, pl.ds(pl.multiple_of(j * t, t), t), :], qs[g])
        if near:
            part = g % B_QUERY_SPLIT
            bias = bias_ref[jnp.clip(j - qi, -reach, reach) + reach]
            return s + bias[:, part * gw:(part + 1) * gw]
        return s + jnp.where(j < qi, val_ref[h, half - 1], val_ref[h, NUM_BUCKETS - 1])

    outs = _bounded_softmax_groups(ngroups, nk, scores, vtp_ref[...])
    _diff_attn_finish(jnp.concatenate(outs, axis=1), t, lq1_ref, lk1_ref, lq2_ref, lk2_ref, sub_ref,
                      o_ref, lam_init)


def _attn_b_kernel(thr_ref, val_ref, q_ref, k_ref, vt_ref, lq1_ref, lk1_ref, lq2_ref, lk2_ref, sub_ref,
                   o_ref, bias_ref, m_ref, acc_ref, s_ref, smax_ref, *, lam_init):
    h = pl.program_id(1)
    qi = pl.program_id(2)
    t = q_ref.shape[3]
    nk = k_ref.shape[2] // t
    half = NUM_BUCKETS // 2
    reach = (BIAS_TILES - 1) // 2

    @pl.when(qi == 0)
    def _bias():
        _build_bias_tiles(thr_ref, val_ref, bias_ref, h, t)

    q = q_ref[0, 0].reshape(2 * t, LANES)
    _init_stats(m_ref, acc_ref)

    n_near, chunk_of = _chunk_order(qi, nk)
    assert nk <= CHUNKS_PER_TRIP or (nk % CHUNKS_PER_TRIP == 0 and CHUNKS_PER_TRIP > n_near)

    def far_shift(j):
        return jnp.where(j < qi, val_ref[h, half - 1], val_ref[h, NUM_BUCKETS - 1])

    def scores(v):
        j, near = chunk_of(v)
        s = _dot_nt(k_ref[0, 0, pl.ds(pl.multiple_of(j * t, t), t), :], q)
        if near:
            bias = bias_ref[jnp.clip(j - qi, -reach, reach) + reach]
            return s + jnp.concatenate([bias, bias], axis=1), 0.0
        return s, far_shift(j)

    def consume(v, s, smax):
        j, near = chunk_of(v)
        vt = vt_ref[0, 0, :, pl.ds(pl.multiple_of(j * t, t), t)]
        _softmax_step(s, smax, vt, 0.0 if near else far_shift(j), m_ref, acc_ref)

    _pipelined_chunks(nk, scores, consume, s_ref, smax_ref)
    _diff_attn_finish(acc_ref[0:B_VDIM, :] / acc_ref[B_VDIM:B_VDIM + 1, :], t,
                      lq1_ref, lk1_ref, lq2_ref, lk2_ref, sub_ref, o_ref, lam_init)


def _attn_b(qb, kb, vbt, thr, vals, lq1, lk1, lq2, lk2, subln_col, lam_init, bounded):
    b_, _, _, s_, _ = qb.shape
    t = min(T_B, s_)
    assert t >= MAX_DISTANCE
    vrows = vbt.shape[2]
    if bounded:
        body = functools.partial(_attn_b_bounded_kernel, lam_init=lam_init)
        scratch = [pltpu.VMEM((BIAS_TILES, t, t), F32), pltpu.VMEM((vrows, s_), BF16)]
    else:
        body = functools.partial(_attn_b_kernel, lam_init=lam_init)
        scratch = [pltpu.VMEM((BIAS_TILES, t, t), F32), pltpu.VMEM((1, 2 * t), F32),
                   pltpu.VMEM((vrows, 2 * t), F32),
                   pltpu.VMEM((2, t, 2 * t), F32), pltpu.VMEM((2, 1, 2 * t), F32)]
    vec = lambda n: pl.BlockSpec((1, n), lambda b, h, i, *_: (0, 0))
    grid_spec = pltpu.PrefetchScalarGridSpec(
        num_scalar_prefetch=2,
        grid=(b_, B_HEADS, s_ // t),
        in_specs=[
            pl.BlockSpec((1, 1, 2, t, LANES), lambda b, h, i, *_: (b, h, 0, i, 0)),
            pl.BlockSpec((1, 1, s_, LANES), lambda b, h, i, *_: (b, h, 0, 0)),
            pl.BlockSpec((1, 1, vrows, s_), lambda b, h, i, *_: (b, h, 0, 0)),
            vec(HEAD_DIM), vec(HEAD_DIM), vec(HEAD_DIM), vec(HEAD_DIM),
            pl.BlockSpec((B_VDIM, 1), lambda b, h, i, *_: (0, 0)),
        ],
        out_specs=pl.BlockSpec((1, t, B_VDIM), lambda b, h, i, *_: (b, i, h)),
        scratch_shapes=scratch,
    )
    return pl.pallas_call(
        body,
        grid_spec=grid_spec,
        out_shape=jax.ShapeDtypeStruct((b_, s_, B_WIDTH), BF16),
        compiler_params=_cparams(("parallel", "parallel", "arbitrary")),
        name="attn_b_bounded" if bounded else "attn_b",
    )(thr, vals, qb, kb, vbt, lq1, lk1, lq2, lk2, subln_col)


def _outproj_kernel(oa_ref, ob_ref, x_ref, wa_ref, wb_ref, g_ref, b_ref, r_ref,
                    x1_ref, x1b_ref, aff_ref, *, alpha):
    tm = x_ref.shape[1]
    mix = _dot(oa_ref[0], wa_ref[...]) + _dot(ob_ref[0], wb_ref[...])
    x1 = _layer_norm(alpha * x_ref[0] + mix, g_ref[...], b_ref[...])
    x1_ref[0] = x1
    x1b_ref[0] = x1.astype(BF16)
    hi, lo = _split_bf16(x1)
    parts = _dot(hi, r_ref[...]) + _dot(lo, r_ref[...])
    logits = parts[:, 0:LANES] + parts[:, LANES:2 * LANES]
    lane = lax.broadcasted_iota(I32, (tm, LANES), 1)
    logits = jnp.where(lane < N_EXPERTS, logits, NEG_BIG)
    e = jnp.exp(logits - jnp.max(logits, axis=1, keepdims=True))
    aff = e / jnp.sum(e, axis=1, keepdims=True)
    aff_t = aff.T
    for c in range(tm // LANES):
        aff_ref[0, c] = aff_t[0:N_EXPERTS, c * LANES:(c + 1) * LANES]


def _outproj(oa, ob, x, wo_a, wo_b, g, b, r_hi_lo, alpha):
    b_, s_, d_ = x.shape
    tm = min(TM_PROJ, s_)
    nch = tm // LANES
    full = lambda shape: pl.BlockSpec(shape, lambda b, i: (0,) * len(shape))
    return pl.pallas_call(
        functools.partial(_outproj_kernel, alpha=alpha),
        grid=(b_, s_ // tm),
        in_specs=[
            pl.BlockSpec((1, tm, A_Q), lambda b, i: (b, i, 0)),
            pl.BlockSpec((1, tm, B_WIDTH), lambda b, i: (b, i, 0)),
            pl.BlockSpec((1, tm, d_), lambda b, i: (b, i, 0)),
            full((A_Q, d_)), full((B_WIDTH, d_)), full((1, d_)), full((1, d_)),
            full((d_, 2 * LANES)),
        ],
        out_specs=[
            pl.BlockSpec((1, tm, d_), lambda b, i: (b, i, 0)),
            pl.BlockSpec((1, tm, d_), lambda b, i: (b, i, 0)),
            pl.BlockSpec((1, nch, N_EXPERTS, LANES), lambda b, i: (b, i, 0, 0)),
        ],
        out_shape=[
            jax.ShapeDtypeStruct((b_, s_, d_), F32),
            jax.ShapeDtypeStruct((b_, s_, d_), BF16),
            jax.ShapeDtypeStruct((b_, s_ // LANES, N_EXPERTS, LANES), F32),
        ],
        compiler_params=_cparams(("parallel", "parallel")),
        name="outproj",
    )(oa, ob, x, wo_a, wo_b, g, b, r_hi_lo)


def _route_kernel(aff_ref, tri_ref, pos_ref, gate_ref, off_ref, *, cap):
    nc = aff_ref.shape[1]
    aff = aff_ref[0]
    bits = pltpu.bitcast(aff, I32)

    def count(mask):
        per_lane = jnp.sum(mask.astype(F32), axis=0)
        return jnp.sum(per_lane, axis=1, keepdims=True)

    thr = jnp.zeros((N_EXPERTS, 1), I32)
    for bit in range(30, -1, -1):
        cand = thr | (1 << bit)
        thr = jnp.where(count(bits >= cand[None]) >= cap, cand, thr)

    gt = bits > thr[None]
    eq = bits == thr[None]
    need = cap - count(gt)

    def prefix(mask, out_ref, extra_ref):
        mb = mask.astype(BF16).reshape(nc * N_EXPERTS, LANES)
        incl = _dot(mb, tri_ref[...]).reshape(nc, N_EXPERTS, LANES)
        excl = incl - mask.astype(F32)
        tot = incl[:, :, LANES - 1:LANES]

        def body(c, off):
            out_ref[0, c] = (excl[c] + off).astype(I32)
            if extra_ref is not None:
                extra_ref[0, c] = jnp.broadcast_to(off, (N_EXPERTS, LANES)).astype(I32)
            return off + tot[c]

        off = jnp.zeros((N_EXPERTS, 1), F32)
        for c in range(nc):
            off = body(c, off)

    prefix(eq, pos_ref, None)
    sel = gt | (eq & (pos_ref[0] < need[None].astype(I32)))
    prefix(sel, pos_ref, off_ref)
    pos_ref[0] = jnp.where(sel, pos_ref[0], -1)
    gate_ref[0] = jnp.where(sel, aff, 0.0)


def _route(aff, tri, cap):
    b_, nc, _, _ = aff.shape
    blk = pl.BlockSpec((1, nc, N_EXPERTS, LANES), lambda b: (b, 0, 0, 0))
    return pl.pallas_call(
        functools.partial(_route_kernel, cap=cap),
        grid=(b_,),
        in_specs=[blk, pl.BlockSpec((LANES, LANES), lambda b: (0, 0))],
        out_specs=[blk, blk, blk],
        out_shape=[jax.ShapeDtypeStruct(aff.shape, I32), jax.ShapeDtypeStruct(aff.shape, F32),
                   jax.ShapeDtypeStruct(aff.shape, I32)],
        compiler_params=_cparams(("parallel",)),
        name="route",
    )(aff, tri)


def _slab_geometry(starts_ref, b, j, e):
    start = starts_ref[b, j, e]
    count = starts_ref[b, j + 1, e] - start
    base = (start // BF16_ROWS) * BF16_ROWS
    nslab = (start - base + count + SLAB - 1) // SLAB
    return base, jnp.where(count > 0, nslab, 0)


def _dispatch_kernel(starts_ref, x_ref, pos_ref, xe_ref):
    b = pl.program_id(0)
    eg = pl.program_id(1)
    j = pl.program_id(2)
    t = x_ref.shape[1]

    @pl.when(j == 0)
    def _zero():
        xe_ref[...] = jnp.zeros(xe_ref.shape, BF16)

    xt = x_ref[0]
    rid = lax.broadcasted_iota(I32, (SLAB, t), 0)
    geo, local = [], []
    for el in range(E_GROUP):
        e = eg * E_GROUP + el
        geo.append(_slab_geometry(starts_ref, b, j, e))
        prow = jnp.concatenate(
            [pos_ref[0, c, pl.ds(e, 1), :] for c in range(t // LANES)], axis=1)
        local.append(jnp.broadcast_to(prow - geo[el][0], (SLAB, t)))

    def add_rows(el, k, rows):
        dst = pl.ds(pl.multiple_of(geo[el][0] + k * SLAB, BF16_ROWS), SLAB)
        xe_ref[0, el, dst, :] = xe_ref[0, el, dst, :] + rows.astype(BF16)

    onehot = jnp.concatenate([(loc == rid).astype(BF16) for loc in local], axis=0)
    rows = _dot(onehot, xt)
    for el in range(E_GROUP):
        add_rows(el, 0, rows[el * SLAB:(el + 1) * SLAB])

    for el in range(E_GROUP):
        def body(k, carry, el=el):
            add_rows(el, k, _dot((local[el] == rid + k * SLAB).astype(BF16), xt))
            return carry

        lax.fori_loop(1, geo[el][1], body, 0)


def _dispatch(starts, x1b, pos, capp):
    b_, s_, d_ = x1b.shape
    t = min(T_TOK, s_)
    grid_spec = pltpu.PrefetchScalarGridSpec(
        num_scalar_prefetch=1,
        grid=(b_, N_EXPERTS // E_GROUP, s_ // t),
        in_specs=[
            pl.BlockSpec((1, t, d_), lambda b, g, j, *_: (b, j, 0)),
            pl.BlockSpec((1, t // LANES, N_EXPERTS, LANES), lambda b, g, j, *_: (b, j, 0, 0)),
        ],
        out_specs=pl.BlockSpec((1, E_GROUP, capp, d_), lambda b, g, j, *_: (b, g, 0, 0)),
    )
    return pl.pallas_call(
        _dispatch_kernel,
        grid_spec=grid_spec,
        out_shape=jax.ShapeDtypeStruct((b_, N_EXPERTS, capp, d_), BF16),
        compiler_params=_cparams(("parallel", "parallel", "arbitrary")),
        name="dispatch",
    )(starts, x1b, pos)


def _ffn_kernel(xe_ref, wg_ref, wu_ref, wd_ref, y_ref, acc_ref, *, cap):
    e = pl.program_id(0)
    fc = pl.program_id(1)
    b = pl.program_id(2)

    @pl.when((e == 0) & (fc == 0) & (b == 0))
    def _define_acc():
        acc_ref[...] = jnp.zeros(acc_ref.shape, F32)

    xe = xe_ref[0, 0]
    hg = _dot(xe, wg_ref[0].astype(BF16))
    hu = _dot(xe, wu_ref[0].astype(BF16))
    act = (hg * jax.nn.sigmoid(hg) * hu).astype(BF16)
    total = jnp.where(fc > 0, acc_ref[b], 0.0) + _dot(act, wd_ref[0].astype(BF16))
    acc_ref[b] = total
    y_ref[0, 0, 0:cap, :] = total.astype(BF16)
    y_ref[0, 0, cap:, :] = jnp.zeros((y_ref.shape[2] - cap, y_ref.shape[3]), BF16)


def _ffn(xe, w_gate, w_up, w_down, cap):
    b_, ne, capp, d_ = xe.shape
    f_ = w_gate.shape[2]
    fcw = min(FC, f_)
    nf = f_ // fcw
    return pl.pallas_call(
        functools.partial(_ffn_kernel, cap=cap),
        grid=(ne, nf, b_),
        in_specs=[
            pl.BlockSpec((1, 1, cap, d_), lambda e, f, b: (b, e, 0, 0)),
            pl.BlockSpec((1, d_, fcw), lambda e, f, b: (e, 0, f)),
            pl.BlockSpec((1, d_, fcw), lambda e, f, b: (e, 0, f)),
            pl.BlockSpec((1, fcw, d_), lambda e, f, b: (e, f, 0)),
        ],
        out_specs=pl.BlockSpec((1, 1, capp, d_), lambda e, f, b: (jnp.where(f == nf - 1, b, 0), e, 0, 0)),
        out_shape=jax.ShapeDtypeStruct((b_, ne, capp, d_), BF16),
        scratch_shapes=[pltpu.VMEM((b_, cap, d_), F32)],
        compiler_params=_cparams(("arbitrary", "arbitrary", "arbitrary")),
        name="ffn",
    )(xe, w_gate, w_up, w_down)


def _combine_kernel(starts_ref, y_hbm, pos_ref, gate_ref, x1_ref, p_ref, ex_ref, wpg_ref, wpp_ref,
                    g2_ref, b2_ref, g3_ref, b3_ref, o_ref, ybuf, xbuf, sem, xsem, moe_ref, *, alpha):
    b = pl.program_id(0)
    j = pl.program_id(1)
    nt = pl.num_programs(1)
    t = x1_ref.shape[1]
    slot = j % 2

    def first_slab_copy(jj, e, sl):
        base, _ = _slab_geometry(starts_ref, b, jj, e)
        src = y_hbm.at[b, e, pl.ds(pl.multiple_of(base, BF16_ROWS), SLAB), :]
        return pltpu.make_async_copy(src, ybuf.at[sl, pl.ds(e * SLAB, SLAB), :], sem.at[sl, e])

    @pl.when(j == 0)
    def _prime():
        for e in range(N_EXPERTS):
            first_slab_copy(j, e, slot).start()

    @pl.when(j + 1 < nt)
    def _prefetch_next_tile():
        for e in range(N_EXPERTS):
            first_slab_copy(j + 1, e, 1 - slot).start()

    pad = jnp.zeros((LANES - N_EXPERTS, LANES), F32)
    pos_t = jnp.concatenate(
        [jnp.concatenate([pos_ref[0, c].astype(F32), pad], axis=0).T for c in range(t // LANES)], axis=0)
    gate_t = jnp.concatenate(
        [jnp.concatenate([gate_ref[0, c], pad], axis=0).T for c in range(t // LANES)], axis=0)

    geo = [_slab_geometry(starts_ref, b, j, e) for e in range(N_EXPERTS)]
    lane_e = lax.broadcasted_iota(I32, (1, LANES), 1)
    slab_e = lax.broadcasted_iota(I32, (1, N_EXPERTS * SLAB), 1) // SLAB
    start_row = jnp.zeros((1, LANES), F32)
    shift_row = jnp.zeros((1, N_EXPERTS * SLAB), F32)
    for e, (base, _) in enumerate(geo):
        start = starts_ref[b, j, e]
        start_row = jnp.where(lane_e == e, start.astype(F32), start_row)
        shift_row = jnp.where(slab_e == e, (start - base).astype(F32), shift_row)
    rank = jnp.where(pos_t >= 0.0, pos_t - start_row, -512.0).astype(BF16)
    rank_x = _dot(rank, ex_ref[...])
    row_in_slab = (lax.broadcasted_iota(I32, (1, N_EXPERTS * SLAB), 1) % SLAB).astype(F32)
    match = rank_x == row_in_slab - shift_row
    g_hi, g_lo = _split_bf16(gate_t)
    oh_hi = jnp.where(match, _dot(g_hi, ex_ref[...]), 0.0).astype(BF16)
    oh_lo = jnp.where(match, _dot(g_lo, ex_ref[...]), 0.0).astype(BF16)

    for e in range(N_EXPERTS):
        first_slab_copy(j, e, slot).wait()
    ycat = ybuf[slot]
    moe_ref[...] = _dot(oh_hi, ycat) + _dot(oh_lo, ycat)

    cid = lax.broadcasted_iota(I32, (t, SLAB), 1).astype(F32)
    for e, (base, nslab) in enumerate(geo):
        def body(k, carry, e=e, base=base):
            src = y_hbm.at[b, e, pl.ds(pl.multiple_of(base + k * SLAB, BF16_ROWS), SLAB), :]
            cp = pltpu.make_async_copy(src, xbuf, xsem)
            cp.start()
            local = jnp.broadcast_to(pos_t[:, e:e + 1] - (base + k * SLAB).astype(F32), (t, SLAB))
            onehot = (local == cid).astype(BF16)
            cp.wait()
            moe_ref[...] = moe_ref[...] + gate_t[:, e:e + 1] * _dot(onehot, xbuf[...])
            return carry

        lax.fori_loop(1, nslab, body, 0)

    x2 = _layer_norm(alpha * x1_ref[0] + moe_ref[...], g2_ref[...], b2_ref[...])
    gate = jax.nn.sigmoid(_dot(x2.astype(BF16), wpg_ref[...]))
    ple = _dot(p_ref[0].astype(BF16), wpp_ref[...]) * gate
    o_ref[0] = _layer_norm(alpha * x2 + ple, g3_ref[...], b3_ref[...])


def _combine(starts, y, pos, gate, x1, p, wpg, wpp, g2, b2, g3, b3, alpha):
    b_, s_, d_ = x1.shape
    t = min(T_TOK, s_)
    pd = p.shape[2]
    ex = (jnp.arange(LANES)[:, None] == jnp.arange(N_EXPERTS * SLAB)[None, :] // SLAB).astype(BF16)
    full = lambda shape: pl.BlockSpec(shape, lambda b, j, *_: (0,) * len(shape))
    tile4 = pl.BlockSpec((1, t // LANES, N_EXPERTS, LANES), lambda b, j, *_: (b, j, 0, 0))
    grid_spec = pltpu.PrefetchScalarGridSpec(
        num_scalar_prefetch=1,
        grid=(b_, s_ // t),
        in_specs=[
            pl.BlockSpec(memory_space=pl.ANY),
            tile4, tile4,
            pl.BlockSpec((1, t, d_), lambda b, j, *_: (b, j, 0)),
            pl.BlockSpec((1, t, pd), lambda b, j, *_: (b, j, 0)),
            full((LANES, N_EXPERTS * SLAB)),
            full((d_, d_)), full((pd, d_)),
            full((1, d_)), full((1, d_)), full((1, d_)), full((1, d_)),
        ],
        out_specs=pl.BlockSpec((1, t, d_), lambda b, j, *_: (b, j, 0)),
        scratch_shapes=[pltpu.VMEM((2, N_EXPERTS * SLAB, d_), BF16),
                        pltpu.VMEM((SLAB, d_), BF16),
                        pltpu.SemaphoreType.DMA((2, N_EXPERTS)),
                        pltpu.SemaphoreType.DMA(()),
                        pltpu.VMEM((t, d_), F32)],
    )
    return pl.pallas_call(
        functools.partial(_combine_kernel, alpha=alpha),
        grid_spec=grid_spec,
        out_shape=jax.ShapeDtypeStruct((b_, s_, d_), F32),
        compiler_params=_cparams(("parallel", "arbitrary")),
        name="combine",
    )(starts, y, pos, gate, x1, p, ex, wpg, wpp, g2, b2, g3, b3)


def _rope_tables(s_):
    half = HEAD_DIM // 2
    inv = ROPE_THETA ** (-jnp.arange(0, half, 2, dtype=F32) / half)
    t = jnp.arange(s_, dtype=jnp.int32)
    row = (t // GRID_W).astype(F32)[:, None] * inv[None, :]
    col = (t % GRID_W).astype(F32)[:, None] * inv[None, :]
    cos = jnp.concatenate([jnp.cos(row), jnp.cos(row), jnp.cos(col), jnp.cos(col)], axis=1)
    sin = jnp.concatenate([-jnp.sin(row), jnp.sin(row), -jnp.sin(col), jnp.sin(col)], axis=1)
    reps = LANES // HEAD_DIM
    return jnp.tile(cos, (1, reps)), jnp.tile(sin, (1, reps))


def _t5_bucket(rel):
    half = NUM_BUCKETS // 2
    max_exact = half // 2
    ret = (rel > 0).astype(jnp.int32) * half
    n = jnp.abs(rel)
    nf = jnp.maximum(n, 1).astype(F32)
    large = max_exact + (jnp.log(nf / max_exact) / math.log(MAX_DISTANCE / max_exact)
                         * (half - max_exact)).astype(jnp.int32)
    large = jnp.minimum(large, half - 1)
    return ret + jnp.where(n < max_exact, n, large)


def _bucket_thresholds():
    half = NUM_BUCKETS // 2
    n = jnp.arange(0, MAX_DISTANCE + 1, dtype=jnp.int32)
    bk = _t5_bucket(-n)
    j = jnp.arange(half, dtype=jnp.int32)
    return jnp.sum((bk[None, :] < j[:, None]).astype(jnp.int32), axis=1)


def _group_mean_matrix(width):
    g = jnp.arange(width, dtype=jnp.int32) // HEAD_DIM
    return ((g[:, None] == g[None, :]).astype(F32) / HEAD_DIM).astype(BF16)


def kernel(x, p, w_in, w_out, a_q_norm, a_k_norm, b_lambda_q1, b_lambda_k1, b_lambda_q2, b_lambda_k2,
           b_subln, rel_bias, ln1_g, ln1_b, w_router, w_gate, w_up, w_down, ln2_g, ln2_b,
           w_ple_gate, w_ple_proj, ln3_g, ln3_b):
    b_, s_, d_ = x.shape
    depth = w_in.shape[0]
    alpha = (2 * depth) ** 0.25
    cap = EC_CAPACITY_FACTOR * s_ // N_EXPERTS
    t_tok = min(T_TOK, s_)
    capp = cap + SLAB
    assert s_ % GRID_W == 0 and s_ % LANES == 0 and cap % BF16_ROWS == 0

    cs, sn = _rope_tables(s_)
    thr = _bucket_thresholds()
    bias_vals = (rel_bias.astype(F32) * LOG2E).T
    gmq, gmk = _group_mean_matrix(A_Q), _group_mean_matrix(A_KV)
    tri = (jnp.arange(LANES)[:, None] <= jnp.arange(LANES)[None, :]).astype(BF16)
    row = lambda v: v.astype(F32).reshape(1, -1)

    for i in range(depth):
        lam_init = 0.8 - 0.6 * math.exp(-0.3 * i)
        qa, ka, vat, qb, kb, vbt, qb_sq, kb_sq = _inproj(
            x, w_in[i].astype(BF16), cs, sn,
            jnp.tile(row(a_q_norm[i]), (1, A_HEADS)), jnp.tile(row(a_k_norm[i]), (1, A_KV_HEADS)), gmq, gmk)
        bound_a = (HEAD_DIM * ATTN_SCALE * LOG2E * 1.01
                   * jnp.max(jnp.abs(a_q_norm[i].astype(F32))) * jnp.max(jnp.abs(a_k_norm[i].astype(F32))))
        bound_b = 1.03 * jnp.sqrt(jnp.max(qb_sq) * jnp.max(kb_sq)) + jnp.max(jnp.abs(bias_vals))
        unrollable_a = (s_ // min(TK_A, s_)) * A_REP <= MAX_UNROLLED_BLOCKS
        unrollable_b = (s_ // min(T_B, s_)) * 2 * B_QUERY_SPLIT <= MAX_UNROLLED_BLOCKS
        oa = lax.cond(unrollable_a & (bound_a <= EXP_RANGE),
                      lambda *a: _attn_a(*a, bounded=True), lambda *a: _attn_a(*a, bounded=False), qa, ka, vat)
        b_args = (qb, kb, vbt, thr, bias_vals, row(b_lambda_q1[i]), row(b_lambda_k1[i]),
                  row(b_lambda_q2[i]), row(b_lambda_k2[i]), b_subln[i].astype(F32).reshape(-1, 1))
        ob = lax.cond(unrollable_b & (bound_b <= EXP_RANGE),
                      lambda *a: _attn_b(*a, lam_init, bounded=True),
                      lambda *a: _attn_b(*a, lam_init, bounded=False), *b_args)
        wr = jnp.pad(w_router[i].astype(F32), ((0, 0), (0, LANES - N_EXPERTS)))
        rh = wr.astype(BF16)
        rl = (wr - rh.astype(F32)).astype(BF16)
        wo = w_out[i].astype(BF16)
        x1, x1b, aff = _outproj(oa, ob, x, wo[:A_Q], wo[A_Q:], row(ln1_g[i]), row(ln1_b[i]),
                                jnp.concatenate([rh, rl], axis=1), alpha)
        pos, gate, off = _route(aff, tri, cap)
        starts = jnp.concatenate(
            [off[:, ::t_tok // LANES, :, 0], jnp.full((b_, 1, N_EXPERTS), cap, jnp.int32)], axis=1)
        xe = _dispatch(starts, x1b, pos, capp)
        y = _ffn(xe, w_gate[i], w_up[i], w_down[i], cap)
        x = _combine(starts, y, pos, gate, x1, p[i], w_ple_gate[i].astype(BF16), w_ple_proj[i].astype(BF16),
                     row(ln2_g[i]), row(ln2_b[i]), row(ln3_g[i]), row(ln3_b[i]), alpha)
    return x
```

```python
import functools
import math

import jax
import jax.numpy as jnp
from jax import lax
from jax.experimental import pallas as pl
from jax.experimental.pallas import tpu as pltpu

F32 = jnp.float32
BF16 = jnp.bfloat16
I32 = jnp.int32

HEAD_DIM = 64
A_HEADS = 8
A_KV_HEADS = 2
A_REP = A_HEADS // A_KV_HEADS
B_HEADS = 4
B_VDIM = 2 * HEAD_DIM
A_Q = A_HEADS * HEAD_DIM
A_KV = A_KV_HEADS * HEAD_DIM
B_QK = B_HEADS * 2 * HEAD_DIM
B_WIDTH = B_HEADS * B_VDIM
ATTN_SCALE = HEAD_DIM ** -0.5
GRID_W = 64
ROPE_THETA = 10000.0
NUM_BUCKETS = 32
MAX_DISTANCE = 128
N_EXPERTS = 16
EC_CAPACITY_FACTOR = 2
LN_EPS = 1e-5
QK_EPS = 1e-6
LOG2E = math.log2(math.e)

LANES = 128
BF16_ROWS = 16
ONES_ROWS = BF16_ROWS
VMEM_LIMIT = 56 * 1024 * 1024

TM_PROJ = 512
TQ_A = 256
TK_A = 512
T_B = 512
SCORE_LOOKAHEAD = 3
MAX_UNROLLED_BLOCKS = 128
TQ_A_BOUNDED = 512
A_QUERY_SPLIT = 2
B_QUERY_SPLIT = 2
CHUNKS_PER_TRIP = 4
BIAS_TILES = 5
T_TOK = 256
SLAB = 64
E_GROUP = 8
FC = 512
NEG_BIG = -1e30
EXP_RANGE = 40.0


def _cparams(sem):
    return pltpu.CompilerParams(dimension_semantics=sem, vmem_limit_bytes=VMEM_LIMIT)


def _dot(a, b):
    return jnp.dot(a, b, preferred_element_type=F32)


def _dot_nt(a, b):
    return lax.dot_general(a, b, (((1,), (1,)), ((), ())), preferred_element_type=F32)


def _layer_norm(y, g, b):
    mu = jnp.mean(y, axis=-1, keepdims=True)
    yc = y - mu
    var = jnp.mean(yc * yc, axis=-1, keepdims=True)
    return yc * lax.rsqrt(var + LN_EPS) * g + b


def _split_bf16(v):
    hi = v.astype(BF16)
    lo = (v - hi.astype(F32)).astype(BF16)
    return hi, lo


def _inproj_kernel(x_ref, w_ref, wvt_ref, cs_ref, sn_ref, gq_ref, gk_ref, gmq_ref, gmk_ref, gsum_ref,
                   qa_ref, ka_ref, vat_ref, qb_ref, kb_ref, vbt_ref, qn_ref, kn_ref):
    tm = x_ref.shape[1]
    xb = x_ref[0].astype(BF16)
    lane = lax.broadcasted_iota(I32, (tm, LANES), 1)
    lo_half = lane < HEAD_DIM
    cs = cs_ref[...]
    sn = sn_ref[...]
    qscale = ATTN_SCALE * LOG2E
    ones_rows = (lax.broadcasted_iota(I32, (ONES_ROWS, tm), 0) == 0).astype(BF16)

    def group_rms(v, gm_ref, gain):
        hi, lo = _split_bf16(v * v)
        ms = _dot(hi, gm_ref[...]) + _dot(lo, gm_ref[...])
        return v * lax.rsqrt(ms + QK_EPS) * gain

    def rope(v, reps):
        width = v.shape[1]
        lane_w = lax.broadcasted_iota(I32, v.shape, 1)
        first = (lane_w % (HEAD_DIM // 2)) < (HEAD_DIM // 4)
        rot = jnp.where(first, pltpu.roll(v, width - HEAD_DIM // 4, 1), pltpu.roll(v, HEAD_DIM // 4, 1))
        c = jnp.concatenate([cs] * reps, axis=1) if reps > 1 else cs
        s = jnp.concatenate([sn] * reps, axis=1) if reps > 1 else sn
        return v * c + rot * s

    qa = _dot(xb, w_ref[:, 0:A_Q])
    qa = rope(group_rms(qa, gmq_ref, gq_ref[...]), A_Q // LANES) * qscale
    for c in range(A_Q // LANES):
        chunk = qa[:, c * LANES:(c + 1) * LANES]
        swapped = pltpu.roll(chunk, HEAD_DIM, 1)
        if (2 * c) // A_REP == 0:
            even, odd = jnp.where(lo_half, chunk, 0.0), jnp.where(lo_half, swapped, 0.0)
        else:
            even, odd = jnp.where(lo_half, 0.0, swapped), jnp.where(lo_half, 0.0, chunk)
        qa_ref[0, 2 * c] = even.astype(BF16)
        qa_ref[0, 2 * c + 1] = odd.astype(BF16)

    ka = _dot(xb, w_ref[:, A_Q:A_Q + A_KV])
    ka_ref[0] = rope(group_rms(ka, gmk_ref, gk_ref[...]), 1).astype(BF16)
    v_t = _dot_nt(wvt_ref[...], xb).astype(BF16)
    for g in range(A_KV_HEADS):
        vat_ref[0, g] = jnp.concatenate([v_t[g * HEAD_DIM:(g + 1) * HEAD_DIM], ones_rows], axis=0)

    o = A_Q + 2 * A_KV
    qb = _dot(xb, w_ref[:, o:o + B_QK]) * qscale
    kb = _dot(xb, w_ref[:, o + B_QK:o + 2 * B_QK])
    qn_ref[0, 0] = jnp.max(_dot((qb * qb).astype(BF16), gsum_ref[...]), axis=0, keepdims=True)
    kn_ref[0, 0] = jnp.max(_dot((kb * kb).astype(BF16), gsum_ref[...]), axis=0, keepdims=True)
    for h in range(B_HEADS):
        chunk = qb[:, h * LANES:(h + 1) * LANES]
        qb_ref[0, h, 0] = jnp.where(lo_half, chunk, 0.0).astype(BF16)
        qb_ref[0, h, 1] = jnp.where(lo_half, 0.0, chunk).astype(BF16)
        kb_ref[0, h] = kb[:, h * LANES:(h + 1) * LANES].astype(BF16)
        vbt_ref[0, h] = jnp.concatenate(
            [v_t[A_KV + h * B_VDIM:A_KV + (h + 1) * B_VDIM], ones_rows], axis=0)


def _inproj(x, w_in_bf, cs, sn, gq, gk, gmq, gmk):
    b_, s_, d_ = x.shape
    tm = min(TM_PROJ, s_)
    nst = s_ // tm
    in_w = w_in_bf.shape[1]
    vb0 = A_Q + 2 * A_KV + 2 * B_QK
    wv_t = jnp.concatenate([w_in_bf[:, A_Q + A_KV:A_Q + 2 * A_KV], w_in_bf[:, vb0:vb0 + B_WIDTH]], axis=1).T
    gsum = (jnp.arange(B_QK)[:, None] // HEAD_DIM == jnp.arange(LANES)[None, :]).astype(BF16)
    full = lambda shape: pl.BlockSpec(shape, lambda b, i: (0,) * len(shape))
    return pl.pallas_call(
        _inproj_kernel,
        grid=(b_, nst),
        in_specs=[
            pl.BlockSpec((1, tm, d_), lambda b, i: (b, i, 0)),
            full((d_, in_w)),
            full((A_KV + B_WIDTH, d_)),
            pl.BlockSpec((tm, LANES), lambda b, i: (i, 0)),
            pl.BlockSpec((tm, LANES), lambda b, i: (i, 0)),
            full((1, A_Q)), full((1, A_KV)), full((A_Q, A_Q)), full((A_KV, A_KV)), full((B_QK, LANES)),
        ],
        out_specs=[
            pl.BlockSpec((1, A_HEADS, tm, LANES), lambda b, i: (b, 0, i, 0)),
            pl.BlockSpec((1, tm, LANES), lambda b, i: (b, i, 0)),
            pl.BlockSpec((1, A_KV_HEADS, HEAD_DIM + ONES_ROWS, tm), lambda b, i: (b, 0, 0, i)),
            pl.BlockSpec((1, B_HEADS, 2, tm, LANES), lambda b, i: (b, 0, 0, i, 0)),
            pl.BlockSpec((1, B_HEADS, tm, LANES), lambda b, i: (b, 0, i, 0)),
            pl.BlockSpec((1, B_HEADS, B_VDIM + ONES_ROWS, tm), lambda b, i: (b, 0, 0, i)),
            pl.BlockSpec((1, 1, 1, LANES), lambda b, i: (b, i, 0, 0)),
            pl.BlockSpec((1, 1, 1, LANES), lambda b, i: (b, i, 0, 0)),
        ],
        out_shape=[
            jax.ShapeDtypeStruct((b_, A_HEADS, s_, LANES), BF16),
            jax.ShapeDtypeStruct((b_, s_, LANES), BF16),
            jax.ShapeDtypeStruct((b_, A_KV_HEADS, HEAD_DIM + ONES_ROWS, s_), BF16),
            jax.ShapeDtypeStruct((b_, B_HEADS, 2, s_, LANES), BF16),
            jax.ShapeDtypeStruct((b_, B_HEADS, s_, LANES), BF16),
            jax.ShapeDtypeStruct((b_, B_HEADS, B_VDIM + ONES_ROWS, s_), BF16),
            jax.ShapeDtypeStruct((b_, nst, 1, LANES), F32),
            jax.ShapeDtypeStruct((b_, nst, 1, LANES), F32),
        ],
        compiler_params=_cparams(("parallel", "parallel")),
        name="inproj",
    )(x, w_in_bf, wv_t, cs, sn, gq, gk, gmq, gmk, gsum)


def _softmax_step(s, smax, vt_chunk, shift, m_ref, acc_ref):
    m_prev = m_ref[...]
    m_cur = jnp.maximum(m_prev, smax)
    alpha = jnp.exp2(m_prev - m_cur)
    p = jnp.exp2(s - (m_cur - shift)).astype(BF16)
    acc_ref[...] = alpha * acc_ref[...] + _dot(vt_chunk, p)
    m_ref[...] = m_cur


def _init_stats(m_ref, acc_ref):
    m_ref[...] = jnp.full(m_ref.shape, NEG_BIG, F32)
    acc_ref[...] = jnp.zeros(acc_ref.shape, F32)


def _pipelined_chunks(nk, scores, consume, s_ref, smax_ref):
    assert nk % 2 == 0
    group = CHUNKS_PER_TRIP if nk % CHUNKS_PER_TRIP == 0 else 2

    def produce(j, slot):
        s, shift = scores(j)
        s_ref[slot] = s
        smax_ref[slot] = jnp.max(s, axis=0, keepdims=True) + shift

    def step(j, slot, prefetch):
        if prefetch:
            produce(j + 1, 1 - slot)
        consume(j, s_ref[slot], smax_ref[slot])

    def trip(i, carry):
        for u in range(group):
            step(group * i + u, u % 2, True)
        return carry

    produce(0, 0)
    lax.fori_loop(0, nk // group - 1, trip, 0)
    for u in range(group):
        step(nk - group + u, u % 2, u < group - 1)


def _attn_a_kernel(q_ref, k_ref, vt_ref, o_ref, m_ref, acc_ref, s_ref, smax_ref, *, tk):
    tq = q_ref.shape[2]
    cols = A_REP * tq
    nk = k_ref.shape[1] // tk
    q = q_ref[0].reshape(cols, LANES)
    _init_stats(m_ref, acc_ref)

    def scores(j):
        return _dot_nt(k_ref[0, pl.ds(pl.multiple_of(j * tk, tk), tk), :], q), 0.0

    def consume(j, s, smax):
        vt = vt_ref[0, 0, :, pl.ds(pl.multiple_of(j * tk, tk), tk)]
        _softmax_step(s, smax, vt, 0.0, m_ref, acc_ref)

    _pipelined_chunks(nk, scores, consume, s_ref, smax_ref)
    o = acc_ref[0:HEAD_DIM, :] / acc_ref[HEAD_DIM:HEAD_DIM + 1, :]
    o = jnp.concatenate([o[:, r * tq:(r + 1) * tq] for r in range(A_REP)], axis=0)
    o_ref[0] = o.T.astype(BF16)


def _bounded_softmax_groups(ngroups, nk, scores, values_t):
    vdim = values_t.shape[0] - ONES_ROWS
    blocks = [(g, c) for g in range(ngroups) for c in range(nk)]
    assert len(blocks) <= MAX_UNROLLED_BLOCKS
    inflight = [scores(*blocks[i]) for i in range(min(SCORE_LOOKAHEAD, len(blocks)))]
    out = []
    for g in range(ngroups):
        weights = []
        for c in range(nk):
            ahead = g * nk + c + SCORE_LOOKAHEAD
            if ahead < len(blocks):
                inflight.append(scores(*blocks[ahead]))
            weights.append(jnp.exp2(inflight.pop(0)).astype(BF16))
        acc = _dot(values_t, jnp.concatenate(weights, axis=0))
        out.append(acc[0:vdim] / acc[vdim:vdim + 1])
    return out


def _attn_a_bounded_kernel(q_ref, k_ref, vt_ref, o_ref, *, tk):
    tq = q_ref.shape[2]
    nk = k_ref.shape[1] // tk
    gw = tq // A_QUERY_SPLIT
    qs = [q_ref[0, r, part * gw:(part + 1) * gw, :] for r in range(A_REP) for part in range(A_QUERY_SPLIT)]

    def scores(g, c):
        return _dot_nt(k_ref[0, c * tk:(c + 1) * tk, :], qs[g])

    outs = _bounded_softmax_groups(len(qs), nk, scores, vt_ref[0, 0])
    heads = [jnp.concatenate(outs[r * A_QUERY_SPLIT:(r + 1) * A_QUERY_SPLIT], axis=1) for r in range(A_REP)]
    o_ref[0] = jnp.concatenate(heads, axis=0).T.astype(BF16)


def _attn_a(qa, ka, vat, bounded):
    b_, _, s_, _ = qa.shape
    tq = min(TQ_A_BOUNDED if bounded else TQ_A, s_)
    tk = min(TK_A, s_)
    cols = A_REP * tq
    vrows = vat.shape[2]
    if bounded:
        body = functools.partial(_attn_a_bounded_kernel, tk=tk)
        scratch = []
    else:
        body = functools.partial(_attn_a_kernel, tk=tk)
        scratch = [pltpu.VMEM((1, cols), F32), pltpu.VMEM((vrows, cols), F32),
                   pltpu.VMEM((2, tk, cols), F32), pltpu.VMEM((2, 1, cols), F32)]
    return pl.pallas_call(
        body,
        grid=(b_, A_KV_HEADS, s_ // tq),
        in_specs=[
            pl.BlockSpec((1, A_REP, tq, LANES), lambda b, g, i: (b, g, i, 0)),
            pl.BlockSpec((1, s_, LANES), lambda b, g, i: (b, 0, 0)),
            pl.BlockSpec((1, 1, vrows, s_), lambda b, g, i: (b, g, 0, 0)),
        ],
        out_specs=pl.BlockSpec((1, tq, A_REP * HEAD_DIM), lambda b, g, i: (b, i, g)),
        out_shape=jax.ShapeDtypeStruct((b_, s_, A_Q), BF16),
        scratch_shapes=scratch,
        compiler_params=_cparams(("parallel", "parallel", "arbitrary")),
        name="attn_a_bounded" if bounded else "attn_a",
    )(qa, ka, vat)


def _build_bias_tiles(thr_ref, val_ref, bias_ref, h, t):
    half = NUM_BUCKETS // 2
    reach = (BIAS_TILES - 1) // 2
    key = lax.broadcasted_iota(I32, (t, t), 0)
    qry = lax.broadcasted_iota(I32, (t, t), 1)
    for d in range(-reach, reach + 1):
        rel = key - qry + d * t
        n = jnp.abs(rel)
        neg = jnp.full((t, t), val_ref[h, 0], F32)
        pos = jnp.full((t, t), val_ref[h, half], F32)
        for j in range(1, half):
            ge = n >= thr_ref[j]
            neg = jnp.where(ge, val_ref[h, j], neg)
            pos = jnp.where(ge, val_ref[h, half + j], pos)
        bias_ref[d + reach] = jnp.where(rel > 0, pos, neg)


def _chunk_order(qi, nk):
    reach = (BIAS_TILES - 1) // 2
    n_near = min(nk, 2 * reach - 1)
    near0 = jnp.clip(qi - (n_near // 2), 0, nk - n_near)

    def chunk_of(v):
        if isinstance(v, int) and v >= nk - n_near:
            return near0 + (v - (nk - n_near)), True
        return v + jnp.where(v >= near0, n_near, 0), False

    return n_near, chunk_of


def _diff_attn_finish(o, t, lq1_ref, lk1_ref, lq2_ref, lk2_ref, sub_ref, o_ref, lam_init):
    lam = (jnp.exp(jnp.sum(lq1_ref[...] * lk1_ref[...], axis=1, keepdims=True))
           - jnp.exp(jnp.sum(lq2_ref[...] * lk2_ref[...], axis=1, keepdims=True)) + lam_init)
    o = o[:, 0:t] - lam * o[:, t:2 * t]
    ms = jnp.mean(o * o, axis=0, keepdims=True)
    o = o * lax.rsqrt(ms + LN_EPS) * sub_ref[...] * (1.0 - lam_init)
    o_ref[0] = o.T.astype(BF16)


def _attn_b_bounded_kernel(thr_ref, val_ref, q_ref, k_ref, vt_ref, lq1_ref, lk1_ref, lq2_ref, lk2_ref,
                           sub_ref, o_ref, bias_ref, *, lam_init):
    h = pl.program_id(1)
    qi = pl.program_id(2)
    t = q_ref.shape[3]
    nk = k_ref.shape[2] // t
    reach = (BIAS_TILES - 1) // 2
    gw = t // B_QUERY_SPLIT
    ngroups = 2 * B_QUERY_SPLIT

    @pl.when(qi == 0)
    def _bias():
        _build_bias_tiles(thr_ref, val_ref, bias_ref, h, t)

    qs = [q_ref[0, 0, g // B_QUERY_SPLIT, (g % B_QUERY_SPLIT) * gw:(g % B_QUERY_SPLIT + 1) * gw, :]
          for g in range(ngroups)]

    def scores(g, j):
        part = g % B_QUERY_SPLIT
        bias = bias_ref[jnp.clip(j - qi, -reach, reach) + reach]
        return _dot_nt(k_ref[0, 0, j * t:(j + 1) * t, :], qs[g]) + bias[:, part * gw:(part + 1) * gw]

    outs = _bounded_softmax_groups(ngroups, nk, scores, vt_ref[0, 0])
    _diff_attn_finish(jnp.concatenate(outs, axis=1), t, lq1_ref, lk1_ref, lq2_ref, lk2_ref, sub_ref,
                      o_ref, lam_init)


def _attn_b_kernel(thr_ref, val_ref, q_ref, k_ref, vt_ref, lq1_ref, lk1_ref, lq2_ref, lk2_ref, sub_ref,
                   o_ref, bias_ref, m_ref, acc_ref, s_ref, smax_ref, *, lam_init):
    h = pl.program_id(1)
    qi = pl.program_id(2)
    t = q_ref.shape[3]
    nk = k_ref.shape[2] // t
    half = NUM_BUCKETS // 2
    reach = (BIAS_TILES - 1) // 2

    @pl.when(qi == 0)
    def _bias():
        _build_bias_tiles(thr_ref, val_ref, bias_ref, h, t)

    q = q_ref[0, 0].reshape(2 * t, LANES)
    _init_stats(m_ref, acc_ref)

    n_near, chunk_of = _chunk_order(qi, nk)
    assert nk <= CHUNKS_PER_TRIP or (nk % CHUNKS_PER_TRIP == 0 and CHUNKS_PER_TRIP > n_near)

    def far_shift(j):
        return jnp.where(j < qi, val_ref[h, half - 1], val_ref[h, NUM_BUCKETS - 1])

    def scores(v):
        j, near = chunk_of(v)
        s = _dot_nt(k_ref[0, 0, pl.ds(pl.multiple_of(j * t, t), t), :], q)
        if near:
            bias = bias_ref[jnp.clip(j - qi, -reach, reach) + reach]
            return s + jnp.concatenate([bias, bias], axis=1), 0.0
        return s, far_shift(j)

    def consume(v, s, smax):
        j, near = chunk_of(v)
        vt = vt_ref[0, 0, :, pl.ds(pl.multiple_of(j * t, t), t)]
        _softmax_step(s, smax, vt, 0.0 if near else far_shift(j), m_ref, acc_ref)

    _pipelined_chunks(nk, scores, consume, s_ref, smax_ref)
    _diff_attn_finish(acc_ref[0:B_VDIM, :] / acc_ref[B_VDIM:B_VDIM + 1, :], t,
                      lq1_ref, lk1_ref, lq2_ref, lk2_ref, sub_ref, o_ref, lam_init)


def _attn_b(qb, kb, vbt, thr, vals, lq1, lk1, lq2, lk2, subln_col, lam_init, bounded):
    b_, _, _, s_, _ = qb.shape
    t = min(T_B, s_)
    assert t >= MAX_DISTANCE
    vrows = vbt.shape[2]
    if bounded:
        body = functools.partial(_attn_b_bounded_kernel, lam_init=lam_init)
        scratch = [pltpu.VMEM((BIAS_TILES, t, t), F32)]
    else:
        body = functools.partial(_attn_b_kernel, lam_init=lam_init)
        scratch = [pltpu.VMEM((BIAS_TILES, t, t), F32), pltpu.VMEM((1, 2 * t), F32),
                   pltpu.VMEM((vrows, 2 * t), F32),
                   pltpu.VMEM((2, t, 2 * t), F32), pltpu.VMEM((2, 1, 2 * t), F32)]
    vec = lambda n: pl.BlockSpec((1, n), lambda b, h, i, *_: (0, 0))
    grid_spec = pltpu.PrefetchScalarGridSpec(
        num_scalar_prefetch=2,
        grid=(b_, B_HEADS, s_ // t),
        in_specs=[
            pl.BlockSpec((1, 1, 2, t, LANES), lambda b, h, i, *_: (b, h, 0, i, 0)),
            pl.BlockSpec((1, 1, s_, LANES), lambda b, h, i, *_: (b, h, 0, 0)),
            pl.BlockSpec((1, 1, vrows, s_), lambda b, h, i, *_: (b, h, 0, 0)),
            vec(HEAD_DIM), vec(HEAD_DIM), vec(HEAD_DIM), vec(HEAD_DIM),
            pl.BlockSpec((B_VDIM, 1), lambda b, h, i, *_: (0, 0)),
        ],
        out_specs=pl.BlockSpec((1, t, B_VDIM), lambda b, h, i, *_: (b, i, h)),
        scratch_shapes=scratch,
    )
    return pl.pallas_call(
        body,
        grid_spec=grid_spec,
        out_shape=jax.ShapeDtypeStruct((b_, s_, B_WIDTH), BF16),
        compiler_params=_cparams(("parallel", "parallel", "arbitrary")),
        name="attn_b_bounded" if bounded else "attn_b",
    )(thr, vals, qb, kb, vbt, lq1, lk1, lq2, lk2, subln_col)


def _outproj_kernel(oa_ref, ob_ref, x_ref, wa_ref, wb_ref, g_ref, b_ref, r_ref,
                    x1_ref, x1b_ref, aff_ref, *, alpha):
    tm = x_ref.shape[1]
    mix = _dot(oa_ref[0], wa_ref[...]) + _dot(ob_ref[0], wb_ref[...])
    x1 = _layer_norm(alpha * x_ref[0] + mix, g_ref[...], b_ref[...])
    x1_ref[0] = x1
    x1b_ref[0] = x1.astype(BF16)
    hi, lo = _split_bf16(x1)
    parts = _dot(hi, r_ref[...]) + _dot(lo, r_ref[...])
    logits = parts[:, 0:LANES] + parts[:, LANES:2 * LANES]
    lane = lax.broadcasted_iota(I32, (tm, LANES), 1)
    logits = jnp.where(lane < N_EXPERTS, logits, NEG_BIG)
    e = jnp.exp(logits - jnp.max(logits, axis=1, keepdims=True))
    aff = e / jnp.sum(e, axis=1, keepdims=True)
    aff_t = aff.T
    for c in range(tm // LANES):
        aff_ref[0, c] = aff_t[0:N_EXPERTS, c * LANES:(c + 1) * LANES]


def _outproj(oa, ob, x, wo_a, wo_b, g, b, r_hi_lo, alpha):
    b_, s_, d_ = x.shape
    tm = min(TM_PROJ, s_)
    nch = tm // LANES
    full = lambda shape: pl.BlockSpec(shape, lambda b, i: (0,) * len(shape))
    return pl.pallas_call(
        functools.partial(_outproj_kernel, alpha=alpha),
        grid=(b_, s_ // tm),
        in_specs=[
            pl.BlockSpec((1, tm, A_Q), lambda b, i: (b, i, 0)),
            pl.BlockSpec((1, tm, B_WIDTH), lambda b, i: (b, i, 0)),
            pl.BlockSpec((1, tm, d_), lambda b, i: (b, i, 0)),
            full((A_Q, d_)), full((B_WIDTH, d_)), full((1, d_)), full((1, d_)),
            full((d_, 2 * LANES)),
        ],
        out_specs=[
            pl.BlockSpec((1, tm, d_), lambda b, i: (b, i, 0)),
            pl.BlockSpec((1, tm, d_), lambda b, i: (b, i, 0)),
            pl.BlockSpec((1, nch, N_EXPERTS, LANES), lambda b, i: (b, i, 0, 0)),
        ],
        out_shape=[
            jax.ShapeDtypeStruct((b_, s_, d_), F32),
            jax.ShapeDtypeStruct((b_, s_, d_), BF16),
            jax.ShapeDtypeStruct((b_, s_ // LANES, N_EXPERTS, LANES), F32),
        ],
        compiler_params=_cparams(("parallel", "parallel")),
        name="outproj",
    )(oa, ob, x, wo_a, wo_b, g, b, r_hi_lo)


def _route_kernel(aff_ref, tri_ref, pos_ref, gate_ref, off_ref, *, cap):
    nc = aff_ref.shape[1]
    aff = aff_ref[0]
    bits = pltpu.bitcast(aff, I32)

    def count(mask):
        per_lane = jnp.sum(mask.astype(F32), axis=0)
        return jnp.sum(per_lane, axis=1, keepdims=True)

    thr = jnp.zeros((N_EXPERTS, 1), I32)
    for bit in range(30, -1, -1):
        cand = thr | (1 << bit)
        thr = jnp.where(count(bits >= cand[None]) >= cap, cand, thr)

    gt = bits > thr[None]
    eq = bits == thr[None]
    need = cap - count(gt)

    def prefix(mask, out_ref, extra_ref):
        mb = mask.astype(BF16).reshape(nc * N_EXPERTS, LANES)
        incl = _dot(mb, tri_ref[...]).reshape(nc, N_EXPERTS, LANES)
        excl = incl - mask.astype(F32)
        tot = incl[:, :, LANES - 1:LANES]

        def body(c, off):
            out_ref[0, c] = (excl[c] + off).astype(I32)
            if extra_ref is not None:
                extra_ref[0, c] = jnp.broadcast_to(off, (N_EXPERTS, LANES)).astype(I32)
            return off + tot[c]

        off = jnp.zeros((N_EXPERTS, 1), F32)
        for c in range(nc):
            off = body(c, off)

    prefix(eq, pos_ref, None)
    sel = gt | (eq & (pos_ref[0] < need[None].astype(I32)))
    prefix(sel, pos_ref, off_ref)
    pos_ref[0] = jnp.where(sel, pos_ref[0], -1)
    gate_ref[0] = jnp.where(sel, aff, 0.0)


def _route(aff, tri, cap):
    b_, nc, _, _ = aff.shape
    blk = pl.BlockSpec((1, nc, N_EXPERTS, LANES), lambda b: (b, 0, 0, 0))
    return pl.pallas_call(
        functools.partial(_route_kernel, cap=cap),
        grid=(b_,),
        in_specs=[blk, pl.BlockSpec((LANES, LANES), lambda b: (0, 0))],
        out_specs=[blk, blk, blk],
        out_shape=[jax.ShapeDtypeStruct(aff.shape, I32), jax.ShapeDtypeStruct(aff.shape, F32),
                   jax.ShapeDtypeStruct(aff.shape, I32)],
        compiler_params=_cparams(("parallel",)),
        name="route",
    )(aff, tri)


def _slab_geometry(starts_ref, b, j, e):
    start = starts_ref[b, j, e]
    count = starts_ref[b, j + 1, e] - start
    base = (start // BF16_ROWS) * BF16_ROWS
    nslab = (start - base + count + SLAB - 1) // SLAB
    return base, jnp.where(count > 0, nslab, 0)


def _dispatch_kernel(starts_ref, x_ref, pos_ref, xe_ref):
    b = pl.program_id(0)
    eg = pl.program_id(1)
    j = pl.program_id(2)
    t = x_ref.shape[1]

    @pl.when(j == 0)
    def _zero():
        xe_ref[...] = jnp.zeros(xe_ref.shape, BF16)

    xt = x_ref[0]
    rid = lax.broadcasted_iota(I32, (SLAB, t), 0)
    geo, local = [], []
    for el in range(E_GROUP):
        e = eg * E_GROUP + el
        geo.append(_slab_geometry(starts_ref, b, j, e))
        prow = jnp.concatenate(
            [pos_ref[0, c, pl.ds(e, 1), :] for c in range(t // LANES)], axis=1)
        local.append(jnp.broadcast_to(prow - geo[el][0], (SLAB, t)))

    def add_rows(el, k, rows):
        dst = pl.ds(pl.multiple_of(geo[el][0] + k * SLAB, BF16_ROWS), SLAB)
        xe_ref[0, el, dst, :] = xe_ref[0, el, dst, :] + rows.astype(BF16)

    onehot = jnp.concatenate([(loc == rid).astype(BF16) for loc in local], axis=0)
    rows = _dot(onehot, xt)
    for el in range(E_GROUP):
        add_rows(el, 0, rows[el * SLAB:(el + 1) * SLAB])

    for el in range(E_GROUP):
        def body(k, carry, el=el):
            add_rows(el, k, _dot((local[el] == rid + k * SLAB).astype(BF16), xt))
            return carry

        lax.fori_loop(1, geo[el][1], body, 0)


def _dispatch(starts, x1b, pos, capp):
    b_, s_, d_ = x1b.shape
    t = min(T_TOK, s_)
    grid_spec = pltpu.PrefetchScalarGridSpec(
        num_scalar_prefetch=1,
        grid=(b_, N_EXPERTS // E_GROUP, s_ // t),
        in_specs=[
            pl.BlockSpec((1, t, d_), lambda b, g, j, *_: (b, j, 0)),
            pl.BlockSpec((1, t // LANES, N_EXPERTS, LANES), lambda b, g, j, *_: (b, j, 0, 0)),
        ],
        out_specs=pl.BlockSpec((1, E_GROUP, capp, d_), lambda b, g, j, *_: (b, g, 0, 0)),
    )
    return pl.pallas_call(
        _dispatch_kernel,
        grid_spec=grid_spec,
        out_shape=jax.ShapeDtypeStruct((b_, N_EXPERTS, capp, d_), BF16),
        compiler_params=_cparams(("parallel", "parallel", "arbitrary")),
        name="dispatch",
    )(starts, x1b, pos)


def _ffn_kernel(xe_ref, wg_ref, wu_ref, wd_ref, y_ref, acc_ref, *, cap):
    e = pl.program_id(0)
    fc = pl.program_id(1)
    b = pl.program_id(2)

    @pl.when((e == 0) & (fc == 0) & (b == 0))
    def _define_acc():
        acc_ref[...] = jnp.zeros(acc_ref.shape, F32)

    xe = xe_ref[0, 0]
    hg = _dot(xe, wg_ref[0].astype(BF16))
    hu = _dot(xe, wu_ref[0].astype(BF16))
    act = (hg * jax.nn.sigmoid(hg) * hu).astype(BF16)
    total = jnp.where(fc > 0, acc_ref[b], 0.0) + _dot(act, wd_ref[0].astype(BF16))
    acc_ref[b] = total
    y_ref[0, 0, 0:cap, :] = total.astype(BF16)
    y_ref[0, 0, cap:, :] = jnp.zeros((y_ref.shape[2] - cap, y_ref.shape[3]), BF16)


def _ffn(xe, w_gate, w_up, w_down, cap):
    b_, ne, capp, d_ = xe.shape
    f_ = w_gate.shape[2]
    fcw = min(FC, f_)
    nf = f_ // fcw
    return pl.pallas_call(
        functools.partial(_ffn_kernel, cap=cap),
        grid=(ne, nf, b_),
        in_specs=[
            pl.BlockSpec((1, 1, cap, d_), lambda e, f, b: (b, e, 0, 0)),
            pl.BlockSpec((1, d_, fcw), lambda e, f, b: (e, 0, f)),
            pl.BlockSpec((1, d_, fcw), lambda e, f, b: (e, 0, f)),
            pl.BlockSpec((1, fcw, d_), lambda e, f, b: (e, f, 0)),
        ],
        out_specs=pl.BlockSpec((1, 1, capp, d_), lambda e, f, b: (jnp.where(f == nf - 1, b, 0), e, 0, 0)),
        out_shape=jax.ShapeDtypeStruct((b_, ne, capp, d_), BF16),
        scratch_shapes=[pltpu.VMEM((b_, cap, d_), F32)],
        compiler_params=_cparams(("arbitrary", "arbitrary", "arbitrary")),
        name="ffn",
    )(xe, w_gate, w_up, w_down)


def _combine_kernel(starts_ref, y_hbm, pos_ref, gate_ref, x1_ref, p_ref, ex_ref, wpg_ref, wpp_ref,
                    g2_ref, b2_ref, g3_ref, b3_ref, o_ref, ybuf, xbuf, sem, xsem, moe_ref, *, alpha):
    b = pl.program_id(0)
    j = pl.program_id(1)
    nt = pl.num_programs(1)
    t = x1_ref.shape[1]
    slot = j % 2

    def first_slab_copy(jj, e, sl):
        base, _ = _slab_geometry(starts_ref, b, jj, e)
        src = y_hbm.at[b, e, pl.ds(pl.multiple_of(base, BF16_ROWS), SLAB), :]
        return pltpu.make_async_copy(src, ybuf.at[sl, pl.ds(e * SLAB, SLAB), :], sem.at[sl, e])

    @pl.when(j == 0)
    def _prime():
        for e in range(N_EXPERTS):
            first_slab_copy(j, e, slot).start()

    @pl.when(j + 1 < nt)
    def _prefetch_next_tile():
        for e in range(N_EXPERTS):
            first_slab_copy(j + 1, e, 1 - slot).start()

    pad = jnp.zeros((LANES - N_EXPERTS, LANES), F32)
    pos_t = jnp.concatenate(
        [jnp.concatenate([pos_ref[0, c].astype(F32), pad], axis=0).T for c in range(t // LANES)], axis=0)
    gate_t = jnp.concatenate(
        [jnp.concatenate([gate_ref[0, c], pad], axis=0).T for c in range(t // LANES)], axis=0)

    geo = [_slab_geometry(starts_ref, b, j, e) for e in range(N_EXPERTS)]
    lane_e = lax.broadcasted_iota(I32, (1, LANES), 1)
    slab_e = lax.broadcasted_iota(I32, (1, N_EXPERTS * SLAB), 1) // SLAB
    start_row = jnp.zeros((1, LANES), F32)
    shift_row = jnp.zeros((1, N_EXPERTS * SLAB), F32)
    for e, (base, _) in enumerate(geo):
        start = starts_ref[b, j, e]
        start_row = jnp.where(lane_e == e, start.astype(F32), start_row)
        shift_row = jnp.where(slab_e == e, (start - base).astype(F32), shift_row)
    rank = jnp.where(pos_t >= 0.0, pos_t - start_row, -512.0).astype(BF16)
    rank_x = _dot(rank, ex_ref[...])
    row_in_slab = (lax.broadcasted_iota(I32, (1, N_EXPERTS * SLAB), 1) % SLAB).astype(F32)
    match = rank_x == row_in_slab - shift_row
    g_hi, g_lo = _split_bf16(gate_t)
    oh_hi = jnp.where(match, _dot(g_hi, ex_ref[...]), 0.0).astype(BF16)
    oh_lo = jnp.where(match, _dot(g_lo, ex_ref[...]), 0.0).astype(BF16)

    for e in range(N_EXPERTS):
        first_slab_copy(j, e, slot).wait()
    ycat = ybuf[slot]
    moe_ref[...] = _dot(oh_hi, ycat) + _dot(oh_lo, ycat)

    cid = lax.broadcasted_iota(I32, (t, SLAB), 1).astype(F32)
    for e, (base, nslab) in enumerate(geo):
        def body(k, carry, e=e, base=base):
            src = y_hbm.at[b, e, pl.ds(pl.multiple_of(base + k * SLAB, BF16_ROWS), SLAB), :]
            cp = pltpu.make_async_copy(src, xbuf, xsem)
            cp.start()
            local = jnp.broadcast_to(pos_t[:, e:e + 1] - (base + k * SLAB).astype(F32), (t, SLAB))
            onehot = (local == cid).astype(BF16)
            cp.wait()
            moe_ref[...] = moe_ref[...] + gate_t[:, e:e + 1] * _dot(onehot, xbuf[...])
            return carry

        lax.fori_loop(1, nslab, body, 0)

    x2 = _layer_norm(alpha * x1_ref[0] + moe_ref[...], g2_ref[...], b2_ref[...])
    gate = jax.nn.sigmoid(_dot(x2.astype(BF16), wpg_ref[...]))
    ple = _dot(p_ref[0].astype(BF16), wpp_ref[...]) * gate
    o_ref[0] = _layer_norm(alpha * x2 + ple, g3_ref[...], b3_ref[...])


def _combine(starts, y, pos, gate, x1, p, wpg, wpp, g2, b2, g3, b3, alpha):
    b_, s_, d_ = x1.shape
    t = min(T_TOK, s_)
    pd = p.shape[2]
    ex = (jnp.arange(LANES)[:, None] == jnp.arange(N_EXPERTS * SLAB)[None, :] // SLAB).astype(BF16)
    full = lambda shape: pl.BlockSpec(shape, lambda b, j, *_: (0,) * len(shape))
    tile4 = pl.BlockSpec((1, t // LANES, N_EXPERTS, LANES), lambda b, j, *_: (b, j, 0, 0))
    grid_spec = pltpu.PrefetchScalarGridSpec(
        num_scalar_prefetch=1,
        grid=(b_, s_ // t),
        in_specs=[
            pl.BlockSpec(memory_space=pl.ANY),
            tile4, tile4,
            pl.BlockSpec((1, t, d_), lambda b, j, *_: (b, j, 0)),
            pl.BlockSpec((1, t, pd), lambda b, j, *_: (b, j, 0)),
            full((LANES, N_EXPERTS * SLAB)),
            full((d_, d_)), full((pd, d_)),
            full((1, d_)), full((1, d_)), full((1, d_)), full((1, d_)),
        ],
        out_specs=pl.BlockSpec((1, t, d_), lambda b, j, *_: (b, j, 0)),
        scratch_shapes=[pltpu.VMEM((2, N_EXPERTS * SLAB, d_), BF16),
                        pltpu.VMEM((SLAB, d_), BF16),
                        pltpu.SemaphoreType.DMA((2, N_EXPERTS)),
                        pltpu.SemaphoreType.DMA(()),
                        pltpu.VMEM((t, d_), F32)],
    )
    return pl.pallas_call(
        functools.partial(_combine_kernel, alpha=alpha),
        grid_spec=grid_spec,
        out_shape=jax.ShapeDtypeStruct((b_, s_, d_), F32),
        compiler_params=_cparams(("parallel", "arbitrary")),
        name="combine",
    )(starts, y, pos, gate, x1, p, ex, wpg, wpp, g2, b2, g3, b3)


def _rope_tables(s_):
    half = HEAD_DIM // 2
    inv = ROPE_THETA ** (-jnp.arange(0, half, 2, dtype=F32) / half)
    t = jnp.arange(s_, dtype=jnp.int32)
    row = (t // GRID_W).astype(F32)[:, None] * inv[None, :]
    col = (t % GRID_W).astype(F32)[:, None] * inv[None, :]
    cos = jnp.concatenate([jnp.cos(row), jnp.cos(row), jnp.cos(col), jnp.cos(col)], axis=1)
    sin = jnp.concatenate([-jnp.sin(row), jnp.sin(row), -jnp.sin(col), jnp.sin(col)], axis=1)
    reps = LANES // HEAD_DIM
    return jnp.tile(cos, (1, reps)), jnp.tile(sin, (1, reps))


def _t5_bucket(rel):
    half = NUM_BUCKETS // 2
    max_exact = half // 2
    ret = (rel > 0).astype(jnp.int32) * half
    n = jnp.abs(rel)
    nf = jnp.maximum(n, 1).astype(F32)
    large = max_exact + (jnp.log(nf / max_exact) / math.log(MAX_DISTANCE / max_exact)
                         * (half - max_exact)).astype(jnp.int32)
    large = jnp.minimum(large, half - 1)
    return ret + jnp.where(n < max_exact, n, large)


def _bucket_thresholds():
    half = NUM_BUCKETS // 2
    n = jnp.arange(0, MAX_DISTANCE + 1, dtype=jnp.int32)
    bk = _t5_bucket(-n)
    j = jnp.arange(half, dtype=jnp.int32)
    return jnp.sum((bk[None, :] < j[:, None]).astype(jnp.int32), axis=1)


def _group_mean_matrix(width):
    g = jnp.arange(width, dtype=jnp.int32) // HEAD_DIM
    return ((g[:, None] == g[None, :]).astype(F32) / HEAD_DIM).astype(BF16)


def kernel(x, p, w_in, w_out, a_q_norm, a_k_norm, b_lambda_q1, b_lambda_k1, b_lambda_q2, b_lambda_k2,
           b_subln, rel_bias, ln1_g, ln1_b, w_router, w_gate, w_up, w_down, ln2_g, ln2_b,
           w_ple_gate, w_ple_proj, ln3_g, ln3_b):
    b_, s_, d_ = x.shape
    depth = w_in.shape[0]
    alpha = (2 * depth) ** 0.25
    cap = EC_CAPACITY_FACTOR * s_ // N_EXPERTS
    t_tok = min(T_TOK, s_)
    capp = cap + SLAB
    assert s_ % GRID_W == 0 and s_ % LANES == 0 and cap % BF16_ROWS == 0

    cs, sn = _rope_tables(s_)
    thr = _bucket_thresholds()
    bias_vals = (rel_bias.astype(F32) * LOG2E).T
    gmq, gmk = _group_mean_matrix(A_Q), _group_mean_matrix(A_KV)
    tri = (jnp.arange(LANES)[:, None] <= jnp.arange(LANES)[None, :]).astype(BF16)
    row = lambda v: v.astype(F32).reshape(1, -1)

    for i in range(depth):
        lam_init = 0.8 - 0.6 * math.exp(-0.3 * i)
        qa, ka, vat, qb, kb, vbt, qb_sq, kb_sq = _inproj(
            x, w_in[i].astype(BF16), cs, sn,
            jnp.tile(row(a_q_norm[i]), (1, A_HEADS)), jnp.tile(row(a_k_norm[i]), (1, A_KV_HEADS)), gmq, gmk)
        bound_a = (HEAD_DIM * ATTN_SCALE * LOG2E * 1.01
                   * jnp.max(jnp.abs(a_q_norm[i].astype(F32))) * jnp.max(jnp.abs(a_k_norm[i].astype(F32))))
        bound_b = 1.03 * jnp.sqrt(jnp.max(qb_sq) * jnp.max(kb_sq)) + jnp.max(jnp.abs(bias_vals))
        unrollable_a = (s_ // min(TK_A, s_)) * A_REP * A_QUERY_SPLIT <= MAX_UNROLLED_BLOCKS
        unrollable_b = (s_ // min(T_B, s_)) * 2 * B_QUERY_SPLIT <= MAX_UNROLLED_BLOCKS
        oa = lax.cond(unrollable_a & (bound_a <= EXP_RANGE),
                      lambda *a: _attn_a(*a, bounded=True), lambda *a: _attn_a(*a, bounded=False), qa, ka, vat)
        b_args = (qb, kb, vbt, thr, bias_vals, row(b_lambda_q1[i]), row(b_lambda_k1[i]),
                  row(b_lambda_q2[i]), row(b_lambda_k2[i]), b_subln[i].astype(F32).reshape(-1, 1))
        ob = lax.cond(unrollable_b & (bound_b <= EXP_RANGE),
                      lambda *a: _attn_b(*a, lam_init, bounded=True),
                      lambda *a: _attn_b(*a, lam_init, bounded=False), *b_args)
        wr = jnp.pad(w_router[i].astype(F32), ((0, 0), (0, LANES - N_EXPERTS)))
        rh = wr.astype(BF16)
        rl = (wr - rh.astype(F32)).astype(BF16)
        wo = w_out[i].astype(BF16)
        x1, x1b, aff = _outproj(oa, ob, x, wo[:A_Q], wo[A_Q:], row(ln1_g[i]), row(ln1_b[i]),
                                jnp.concatenate([rh, rl], axis=1), alpha)
        pos, gate, off = _route(aff, tri, cap)
        starts = jnp.concatenate(
            [off[:, ::t_tok // LANES, :, 0], jnp.full((b_, 1, N_EXPERTS), cap, jnp.int32)], axis=1)
        xe = _dispatch(starts, x1b, pos, capp)
        y = _ffn(xe, w_gate[i], w_up[i], w_down[i], cap)
        x = _combine(starts, y, pos, gate, x1, p[i], w_ple_gate[i].astype(BF16), w_ple_proj[i].astype(BF16),
                     row(ln2_g[i]), row(ln2_b[i]), row(ln3_g[i]), row(ln3_b[i]), alpha)
    return x
```

```python
import functools
import math

import jax
import jax.numpy as jnp
from jax import lax
from jax.experimental import pallas as pl
from jax.experimental.pallas import tpu as pltpu

F32 = jnp.float32
BF16 = jnp.bfloat16
I32 = jnp.int32

HEAD_DIM = 64
A_HEADS = 8
A_KV_HEADS = 2
A_REP = A_HEADS // A_KV_HEADS
B_HEADS = 4
B_VDIM = 2 * HEAD_DIM
A_Q = A_HEADS * HEAD_DIM
A_KV = A_KV_HEADS * HEAD_DIM
B_QK = B_HEADS * 2 * HEAD_DIM
B_WIDTH = B_HEADS * B_VDIM
ATTN_SCALE = HEAD_DIM ** -0.5
GRID_W = 64
ROPE_THETA = 10000.0
NUM_BUCKETS = 32
MAX_DISTANCE = 128
N_EXPERTS = 16
EC_CAPACITY_FACTOR = 2
LN_EPS = 1e-5
QK_EPS = 1e-6
LOG2E = math.log2(math.e)

LANES = 128
BF16_ROWS = 16
ONES_ROWS = BF16_ROWS
VMEM_LIMIT = 56 * 1024 * 1024

TM_PROJ = 512
TQ_A = 256
TK_A = 512
T_B = 512
TQ_A_BOUNDED = 512
A_QUERY_SPLIT = 2
B_QUERY_SPLIT = 2
CHUNKS_PER_TRIP = 4
BIAS_TILES = 5
T_TOK = 256
SLAB = 64
E_GROUP = 8
FC = 512
NEG_BIG = -1e30
EXP_RANGE = 40.0


def _cparams(sem):
    return pltpu.CompilerParams(dimension_semantics=sem, vmem_limit_bytes=VMEM_LIMIT)


def _dot(a, b):
    return jnp.dot(a, b, preferred_element_type=F32)


def _dot_nt(a, b):
    return lax.dot_general(a, b, (((1,), (1,)), ((), ())), preferred_element_type=F32)


def _layer_norm(y, g, b):
    mu = jnp.mean(y, axis=-1, keepdims=True)
    yc = y - mu
    var = jnp.mean(yc * yc, axis=-1, keepdims=True)
    return yc * lax.rsqrt(var + LN_EPS) * g + b


def _split_bf16(v):
    hi = v.astype(BF16)
    lo = (v - hi.astype(F32)).astype(BF16)
    return hi, lo


def _inproj_kernel(x_ref, w_ref, wvt_ref, cs_ref, sn_ref, gq_ref, gk_ref, gmq_ref, gmk_ref, gsum_ref,
                   qa_ref, ka_ref, vat_ref, qb_ref, kb_ref, vbt_ref, qn_ref, kn_ref):
    tm = x_ref.shape[1]
    xb = x_ref[0].astype(BF16)
    lane = lax.broadcasted_iota(I32, (tm, LANES), 1)
    lo_half = lane < HEAD_DIM
    cs = cs_ref[...]
    sn = sn_ref[...]
    qscale = ATTN_SCALE * LOG2E
    ones_rows = (lax.broadcasted_iota(I32, (ONES_ROWS, tm), 0) == 0).astype(BF16)

    def group_rms(v, gm_ref, gain):
        hi, lo = _split_bf16(v * v)
        ms = _dot(hi, gm_ref[...]) + _dot(lo, gm_ref[...])
        return v * lax.rsqrt(ms + QK_EPS) * gain

    def rope(v, reps):
        width = v.shape[1]
        lane_w = lax.broadcasted_iota(I32, v.shape, 1)
        first = (lane_w % (HEAD_DIM // 2)) < (HEAD_DIM // 4)
        rot = jnp.where(first, pltpu.roll(v, width - HEAD_DIM // 4, 1), pltpu.roll(v, HEAD_DIM // 4, 1))
        c = jnp.concatenate([cs] * reps, axis=1) if reps > 1 else cs
        s = jnp.concatenate([sn] * reps, axis=1) if reps > 1 else sn
        return v * c + rot * s

    qa = _dot(xb, w_ref[:, 0:A_Q])
    qa = rope(group_rms(qa, gmq_ref, gq_ref[...]), A_Q // LANES) * qscale
    for c in range(A_Q // LANES):
        chunk = qa[:, c * LANES:(c + 1) * LANES]
        swapped = pltpu.roll(chunk, HEAD_DIM, 1)
        if (2 * c) // A_REP == 0:
            even, odd = jnp.where(lo_half, chunk, 0.0), jnp.where(lo_half, swapped, 0.0)
        else:
            even, odd = jnp.where(lo_half, 0.0, swapped), jnp.where(lo_half, 0.0, chunk)
        qa_ref[0, 2 * c] = even.astype(BF16)
        qa_ref[0, 2 * c + 1] = odd.astype(BF16)

    ka = _dot(xb, w_ref[:, A_Q:A_Q + A_KV])
    ka_ref[0] = rope(group_rms(ka, gmk_ref, gk_ref[...]), 1).astype(BF16)
    v_t = _dot_nt(wvt_ref[...], xb).astype(BF16)
    for g in range(A_KV_HEADS):
        vat_ref[0, g] = jnp.concatenate([v_t[g * HEAD_DIM:(g + 1) * HEAD_DIM], ones_rows], axis=0)

    o = A_Q + 2 * A_KV
    qb = _dot(xb, w_ref[:, o:o + B_QK]) * qscale
    kb = _dot(xb, w_ref[:, o + B_QK:o + 2 * B_QK])
    qn_ref[0, 0] = jnp.max(_dot((qb * qb).astype(BF16), gsum_ref[...]), axis=0, keepdims=True)
    kn_ref[0, 0] = jnp.max(_dot((kb * kb).astype(BF16), gsum_ref[...]), axis=0, keepdims=True)
    for h in range(B_HEADS):
        chunk = qb[:, h * LANES:(h + 1) * LANES]
        qb_ref[0, h, 0] = jnp.where(lo_half, chunk, 0.0).astype(BF16)
        qb_ref[0, h, 1] = jnp.where(lo_half, 0.0, chunk).astype(BF16)
        kb_ref[0, h] = kb[:, h * LANES:(h + 1) * LANES].astype(BF16)
        vbt_ref[0, h] = jnp.concatenate(
            [v_t[A_KV + h * B_VDIM:A_KV + (h + 1) * B_VDIM], ones_rows], axis=0)


def _inproj(x, w_in_bf, cs, sn, gq, gk, gmq, gmk):
    b_, s_, d_ = x.shape
    tm = min(TM_PROJ, s_)
    nst = s_ // tm
    in_w = w_in_bf.shape[1]
    vb0 = A_Q + 2 * A_KV + 2 * B_QK
    wv_t = jnp.concatenate([w_in_bf[:, A_Q + A_KV:A_Q + 2 * A_KV], w_in_bf[:, vb0:vb0 + B_WIDTH]], axis=1).T
    gsum = (jnp.arange(B_QK)[:, None] // HEAD_DIM == jnp.arange(LANES)[None, :]).astype(BF16)
    full = lambda shape: pl.BlockSpec(shape, lambda b, i: (0,) * len(shape))
    return pl.pallas_call(
        _inproj_kernel,
        grid=(b_, nst),
        in_specs=[
            pl.BlockSpec((1, tm, d_), lambda b, i: (b, i, 0)),
            full((d_, in_w)),
            full((A_KV + B_WIDTH, d_)),
            pl.BlockSpec((tm, LANES), lambda b, i: (i, 0)),
            pl.BlockSpec((tm, LANES), lambda b, i: (i, 0)),
            full((1, A_Q)), full((1, A_KV)), full((A_Q, A_Q)), full((A_KV, A_KV)), full((B_QK, LANES)),
        ],
        out_specs=[
            pl.BlockSpec((1, A_HEADS, tm, LANES), lambda b, i: (b, 0, i, 0)),
            pl.BlockSpec((1, tm, LANES), lambda b, i: (b, i, 0)),
            pl.BlockSpec((1, A_KV_HEADS, HEAD_DIM + ONES_ROWS, tm), lambda b, i: (b, 0, 0, i)),
            pl.BlockSpec((1, B_HEADS, 2, tm, LANES), lambda b, i: (b, 0, 0, i, 0)),
            pl.BlockSpec((1, B_HEADS, tm, LANES), lambda b, i: (b, 0, i, 0)),
            pl.BlockSpec((1, B_HEADS, B_VDIM + ONES_ROWS, tm), lambda b, i: (b, 0, 0, i)),
            pl.BlockSpec((1, 1, 1, LANES), lambda b, i: (b, i, 0, 0)),
            pl.BlockSpec((1, 1, 1, LANES), lambda b, i: (b, i, 0, 0)),
        ],
        out_shape=[
            jax.ShapeDtypeStruct((b_, A_HEADS, s_, LANES), BF16),
            jax.ShapeDtypeStruct((b_, s_, LANES), BF16),
            jax.ShapeDtypeStruct((b_, A_KV_HEADS, HEAD_DIM + ONES_ROWS, s_), BF16),
            jax.ShapeDtypeStruct((b_, B_HEADS, 2, s_, LANES), BF16),
            jax.ShapeDtypeStruct((b_, B_HEADS, s_, LANES), BF16),
            jax.ShapeDtypeStruct((b_, B_HEADS, B_VDIM + ONES_ROWS, s_), BF16),
            jax.ShapeDtypeStruct((b_, nst, 1, LANES), F32),
            jax.ShapeDtypeStruct((b_, nst, 1, LANES), F32),
        ],
        compiler_params=_cparams(("parallel", "parallel")),
        name="inproj",
    )(x, w_in_bf, wv_t, cs, sn, gq, gk, gmq, gmk, gsum)


def _softmax_step(s, smax, vt_chunk, shift, m_ref, acc_ref):
    m_prev = m_ref[...]
    m_cur = jnp.maximum(m_prev, smax)
    alpha = jnp.exp2(m_prev - m_cur)
    p = jnp.exp2(s - (m_cur - shift)).astype(BF16)
    acc_ref[...] = alpha * acc_ref[...] + _dot(vt_chunk, p)
    m_ref[...] = m_cur


def _init_stats(m_ref, acc_ref):
    m_ref[...] = jnp.full(m_ref.shape, NEG_BIG, F32)
    acc_ref[...] = jnp.zeros(acc_ref.shape, F32)


def _pipelined_chunks(nk, scores, consume, s_ref, smax_ref):
    assert nk % 2 == 0
    group = CHUNKS_PER_TRIP if nk % CHUNKS_PER_TRIP == 0 else 2

    def produce(j, slot):
        s, shift = scores(j)
        s_ref[slot] = s
        smax_ref[slot] = jnp.max(s, axis=0, keepdims=True) + shift

    def step(j, slot, prefetch):
        if prefetch:
            produce(j + 1, 1 - slot)
        consume(j, s_ref[slot], smax_ref[slot])

    def trip(i, carry):
        for u in range(group):
            step(group * i + u, u % 2, True)
        return carry

    produce(0, 0)
    lax.fori_loop(0, nk // group - 1, trip, 0)
    for u in range(group):
        step(nk - group + u, u % 2, u < group - 1)


def _attn_a_kernel(q_ref, k_ref, vt_ref, o_ref, m_ref, acc_ref, s_ref, smax_ref, *, tk):
    tq = q_ref.shape[2]
    cols = A_REP * tq
    nk = k_ref.shape[1] // tk
    q = q_ref[0].reshape(cols, LANES)
    _init_stats(m_ref, acc_ref)

    def scores(j):
        return _dot_nt(k_ref[0, pl.ds(pl.multiple_of(j * tk, tk), tk), :], q), 0.0

    def consume(j, s, smax):
        vt = vt_ref[0, 0, :, pl.ds(pl.multiple_of(j * tk, tk), tk)]
        _softmax_step(s, smax, vt, 0.0, m_ref, acc_ref)

    _pipelined_chunks(nk, scores, consume, s_ref, smax_ref)
    o = acc_ref[0:HEAD_DIM, :] / acc_ref[HEAD_DIM:HEAD_DIM + 1, :]
    o = jnp.concatenate([o[:, r * tq:(r + 1) * tq] for r in range(A_REP)], axis=0)
    o_ref[0] = o.T.astype(BF16)


def _bounded_softmax_groups(ngroups, scores, values_t, bias=None):
    vdim = values_t.shape[0] - ONES_ROWS
    s_next = scores(0)
    out = []
    for g in range(ngroups):
        s = s_next
        if g + 1 < ngroups:
            s_next = scores(g + 1)
        if bias is not None:
            s = bias(g, s)
        acc = _dot(values_t, jnp.exp2(s).astype(BF16))
        out.append(acc[0:vdim] / acc[vdim:vdim + 1])
    return out


def _attn_a_bounded_kernel(q_ref, k_ref, vt_ref, o_ref):
    tq = q_ref.shape[2]
    gw = tq // A_QUERY_SPLIT
    qs = [q_ref[0, r, part * gw:(part + 1) * gw, :] for r in range(A_REP) for part in range(A_QUERY_SPLIT)]
    outs = _bounded_softmax_groups(len(qs), lambda g: _dot_nt(k_ref[0], qs[g]), vt_ref[0, 0])
    heads = [jnp.concatenate(outs[r * A_QUERY_SPLIT:(r + 1) * A_QUERY_SPLIT], axis=1) for r in range(A_REP)]
    o_ref[0] = jnp.concatenate(heads, axis=0).T.astype(BF16)


def _attn_a(qa, ka, vat, bounded):
    b_, _, s_, _ = qa.shape
    tq = min(TQ_A_BOUNDED if bounded else TQ_A, s_)
    tk = min(TK_A, s_)
    cols = A_REP * tq
    vrows = vat.shape[2]
    if bounded:
        body = _attn_a_bounded_kernel
        scratch = []
    else:
        body = functools.partial(_attn_a_kernel, tk=tk)
        scratch = [pltpu.VMEM((1, cols), F32), pltpu.VMEM((vrows, cols), F32),
                   pltpu.VMEM((2, tk, cols), F32), pltpu.VMEM((2, 1, cols), F32)]
    return pl.pallas_call(
        body,
        grid=(b_, A_KV_HEADS, s_ // tq),
        in_specs=[
            pl.BlockSpec((1, A_REP, tq, LANES), lambda b, g, i: (b, g, i, 0)),
            pl.BlockSpec((1, s_, LANES), lambda b, g, i: (b, 0, 0)),
            pl.BlockSpec((1, 1, vrows, s_), lambda b, g, i: (b, g, 0, 0)),
        ],
        out_specs=pl.BlockSpec((1, tq, A_REP * HEAD_DIM), lambda b, g, i: (b, i, g)),
        out_shape=jax.ShapeDtypeStruct((b_, s_, A_Q), BF16),
        scratch_shapes=scratch,
        compiler_params=_cparams(("parallel", "parallel", "arbitrary")),
        name="attn_a_bounded" if bounded else "attn_a",
    )(qa, ka, vat)


def _build_bias_tiles(thr_ref, val_ref, bias_ref, h, t):
    half = NUM_BUCKETS // 2
    reach = (BIAS_TILES - 1) // 2
    key = lax.broadcasted_iota(I32, (t, t), 0)
    qry = lax.broadcasted_iota(I32, (t, t), 1)
    for d in range(-reach, reach + 1):
        rel = key - qry + d * t
        n = jnp.abs(rel)
        neg = jnp.full((t, t), val_ref[h, 0], F32)
        pos = jnp.full((t, t), val_ref[h, half], F32)
        for j in range(1, half):
            ge = n >= thr_ref[j]
            neg = jnp.where(ge, val_ref[h, j], neg)
            pos = jnp.where(ge, val_ref[h, half + j], pos)
        bias_ref[d + reach] = jnp.where(rel > 0, pos, neg)


def _chunk_order(qi, nk):
    reach = (BIAS_TILES - 1) // 2
    n_near = min(nk, 2 * reach - 1)
    near0 = jnp.clip(qi - (n_near // 2), 0, nk - n_near)

    def chunk_of(v):
        if isinstance(v, int) and v >= nk - n_near:
            return near0 + (v - (nk - n_near)), True
        return v + jnp.where(v >= near0, n_near, 0), False

    return n_near, chunk_of


def _diff_attn_finish(o, t, lq1_ref, lk1_ref, lq2_ref, lk2_ref, sub_ref, o_ref, lam_init):
    lam = (jnp.exp(jnp.sum(lq1_ref[...] * lk1_ref[...], axis=1, keepdims=True))
           - jnp.exp(jnp.sum(lq2_ref[...] * lk2_ref[...], axis=1, keepdims=True)) + lam_init)
    o = o[:, 0:t] - lam * o[:, t:2 * t]
    ms = jnp.mean(o * o, axis=0, keepdims=True)
    o = o * lax.rsqrt(ms + LN_EPS) * sub_ref[...] * (1.0 - lam_init)
    o_ref[0] = o.T.astype(BF16)


def _attn_b_bounded_kernel(thr_ref, val_ref, q_ref, k_ref, vt_ref, lq1_ref, lk1_ref, lq2_ref, lk2_ref,
                           sub_ref, o_ref, bias_ref, *, lam_init):
    h = pl.program_id(1)
    qi = pl.program_id(2)
    t = q_ref.shape[3]
    nk = k_ref.shape[2] // t
    reach = (BIAS_TILES - 1) // 2
    gw = t // B_QUERY_SPLIT
    ngroups = 2 * B_QUERY_SPLIT

    @pl.when(qi == 0)
    def _bias():
        _build_bias_tiles(thr_ref, val_ref, bias_ref, h, t)

    qs = [q_ref[0, 0, g // B_QUERY_SPLIT, (g % B_QUERY_SPLIT) * gw:(g % B_QUERY_SPLIT + 1) * gw, :]
          for g in range(ngroups)]

    def add_bias(g, s):
        part = g % B_QUERY_SPLIT
        tiles = [bias_ref[jnp.clip(j - qi, -reach, reach) + reach][:, part * gw:(part + 1) * gw]
                 for j in range(nk)]
        return jnp.concatenate([s[j * t:(j + 1) * t] + tiles[j] for j in range(nk)], axis=0)

    outs = _bounded_softmax_groups(ngroups, lambda g: _dot_nt(k_ref[0, 0], qs[g]), vt_ref[0, 0], add_bias)
    _diff_attn_finish(jnp.concatenate(outs, axis=1), t, lq1_ref, lk1_ref, lq2_ref, lk2_ref, sub_ref,
                      o_ref, lam_init)


def _attn_b_kernel(thr_ref, val_ref, q_ref, k_ref, vt_ref, lq1_ref, lk1_ref, lq2_ref, lk2_ref, sub_ref,
                   o_ref, bias_ref, m_ref, acc_ref, s_ref, smax_ref, *, lam_init):
    h = pl.program_id(1)
    qi = pl.program_id(2)
    t = q_ref.shape[3]
    nk = k_ref.shape[2] // t
    half = NUM_BUCKETS // 2
    reach = (BIAS_TILES - 1) // 2

    @pl.when(qi == 0)
    def _bias():
        _build_bias_tiles(thr_ref, val_ref, bias_ref, h, t)

    q = q_ref[0, 0].reshape(2 * t, LANES)
    _init_stats(m_ref, acc_ref)

    n_near, chunk_of = _chunk_order(qi, nk)
    assert nk <= CHUNKS_PER_TRIP or (nk % CHUNKS_PER_TRIP == 0 and CHUNKS_PER_TRIP > n_near)

    def far_shift(j):
        return jnp.where(j < qi, val_ref[h, half - 1], val_ref[h, NUM_BUCKETS - 1])

    def scores(v):
        j, near = chunk_of(v)
        s = _dot_nt(k_ref[0, 0, pl.ds(pl.multiple_of(j * t, t), t), :], q)
        if near:
            bias = bias_ref[jnp.clip(j - qi, -reach, reach) + reach]
            return s + jnp.concatenate([bias, bias], axis=1), 0.0
        return s, far_shift(j)

    def consume(v, s, smax):
        j, near = chunk_of(v)
        vt = vt_ref[0, 0, :, pl.ds(pl.multiple_of(j * t, t), t)]
        _softmax_step(s, smax, vt, 0.0 if near else far_shift(j), m_ref, acc_ref)

    _pipelined_chunks(nk, scores, consume, s_ref, smax_ref)
    _diff_attn_finish(acc_ref[0:B_VDIM, :] / acc_ref[B_VDIM:B_VDIM + 1, :], t,
                      lq1_ref, lk1_ref, lq2_ref, lk2_ref, sub_ref, o_ref, lam_init)


def _attn_b(qb, kb, vbt, thr, vals, lq1, lk1, lq2, lk2, subln_col, lam_init, bounded):
    b_, _, _, s_, _ = qb.shape
    t = min(T_B, s_)
    assert t >= MAX_DISTANCE
    vrows = vbt.shape[2]
    if bounded:
        body = functools.partial(_attn_b_bounded_kernel, lam_init=lam_init)
        scratch = [pltpu.VMEM((BIAS_TILES, t, t), F32)]
    else:
        body = functools.partial(_attn_b_kernel, lam_init=lam_init)
        scratch = [pltpu.VMEM((BIAS_TILES, t, t), F32), pltpu.VMEM((1, 2 * t), F32),
                   pltpu.VMEM((vrows, 2 * t), F32),
                   pltpu.VMEM((2, t, 2 * t), F32), pltpu.VMEM((2, 1, 2 * t), F32)]
    vec = lambda n: pl.BlockSpec((1, n), lambda b, h, i, *_: (0, 0))
    grid_spec = pltpu.PrefetchScalarGridSpec(
        num_scalar_prefetch=2,
        grid=(b_, B_HEADS, s_ // t),
        in_specs=[
            pl.BlockSpec((1, 1, 2, t, LANES), lambda b, h, i, *_: (b, h, 0, i, 0)),
            pl.BlockSpec((1, 1, s_, LANES), lambda b, h, i, *_: (b, h, 0, 0)),
            pl.BlockSpec((1, 1, vrows, s_), lambda b, h, i, *_: (b, h, 0, 0)),
            vec(HEAD_DIM), vec(HEAD_DIM), vec(HEAD_DIM), vec(HEAD_DIM),
            pl.BlockSpec((B_VDIM, 1), lambda b, h, i, *_: (0, 0)),
        ],
        out_specs=pl.BlockSpec((1, t, B_VDIM), lambda b, h, i, *_: (b, i, h)),
        scratch_shapes=scratch,
    )
    return pl.pallas_call(
        body,
        grid_spec=grid_spec,
        out_shape=jax.ShapeDtypeStruct((b_, s_, B_WIDTH), BF16),
        compiler_params=_cparams(("parallel", "parallel", "arbitrary")),
        name="attn_b_bounded" if bounded else "attn_b",
    )(thr, vals, qb, kb, vbt, lq1, lk1, lq2, lk2, subln_col)


def _outproj_kernel(oa_ref, ob_ref, x_ref, wa_ref, wb_ref, g_ref, b_ref, r_ref,
                    x1_ref, x1b_ref, aff_ref, *, alpha):
    tm = x_ref.shape[1]
    mix = _dot(oa_ref[0], wa_ref[...]) + _dot(ob_ref[0], wb_ref[...])
    x1 = _layer_norm(alpha * x_ref[0] + mix, g_ref[...], b_ref[...])
    x1_ref[0] = x1
    x1b_ref[0] = x1.astype(BF16)
    hi, lo = _split_bf16(x1)
    parts = _dot(hi, r_ref[...]) + _dot(lo, r_ref[...])
    logits = parts[:, 0:LANES] + parts[:, LANES:2 * LANES]
    lane = lax.broadcasted_iota(I32, (tm, LANES), 1)
    logits = jnp.where(lane < N_EXPERTS, logits, NEG_BIG)
    e = jnp.exp(logits - jnp.max(logits, axis=1, keepdims=True))
    aff = e / jnp.sum(e, axis=1, keepdims=True)
    aff_t = aff.T
    for c in range(tm // LANES):
        aff_ref[0, c] = aff_t[0:N_EXPERTS, c * LANES:(c + 1) * LANES]


def _outproj(oa, ob, x, wo_a, wo_b, g, b, r_hi_lo, alpha):
    b_, s_, d_ = x.shape
    tm = min(TM_PROJ, s_)
    nch = tm // LANES
    full = lambda shape: pl.BlockSpec(shape, lambda b, i: (0,) * len(shape))
    return pl.pallas_call(
        functools.partial(_outproj_kernel, alpha=alpha),
        grid=(b_, s_ // tm),
        in_specs=[
            pl.BlockSpec((1, tm, A_Q), lambda b, i: (b, i, 0)),
            pl.BlockSpec((1, tm, B_WIDTH), lambda b, i: (b, i, 0)),
            pl.BlockSpec((1, tm, d_), lambda b, i: (b, i, 0)),
            full((A_Q, d_)), full((B_WIDTH, d_)), full((1, d_)), full((1, d_)),
            full((d_, 2 * LANES)),
        ],
        out_specs=[
            pl.BlockSpec((1, tm, d_), lambda b, i: (b, i, 0)),
            pl.BlockSpec((1, tm, d_), lambda b, i: (b, i, 0)),
            pl.BlockSpec((1, nch, N_EXPERTS, LANES), lambda b, i: (b, i, 0, 0)),
        ],
        out_shape=[
            jax.ShapeDtypeStruct((b_, s_, d_), F32),
            jax.ShapeDtypeStruct((b_, s_, d_), BF16),
            jax.ShapeDtypeStruct((b_, s_ // LANES, N_EXPERTS, LANES), F32),
        ],
        compiler_params=_cparams(("parallel", "parallel")),
        name="outproj",
    )(oa, ob, x, wo_a, wo_b, g, b, r_hi_lo)


def _route_kernel(aff_ref, tri_ref, pos_ref, gate_ref, off_ref, *, cap):
    nc = aff_ref.shape[1]
    aff = aff_ref[0]
    bits = pltpu.bitcast(aff, I32)

    def count(mask):
        per_lane = jnp.sum(mask.astype(F32), axis=0)
        return jnp.sum(per_lane, axis=1, keepdims=True)

    thr = jnp.zeros((N_EXPERTS, 1), I32)
    for bit in range(30, -1, -1):
        cand = thr | (1 << bit)
        thr = jnp.where(count(bits >= cand[None]) >= cap, cand, thr)

    gt = bits > thr[None]
    eq = bits == thr[None]
    need = cap - count(gt)

    def prefix(mask, out_ref, extra_ref):
        mb = mask.astype(BF16).reshape(nc * N_EXPERTS, LANES)
        incl = _dot(mb, tri_ref[...]).reshape(nc, N_EXPERTS, LANES)
        excl = incl - mask.astype(F32)
        tot = incl[:, :, LANES - 1:LANES]

        def body(c, off):
            out_ref[0, c] = (excl[c] + off).astype(I32)
            if extra_ref is not None:
                extra_ref[0, c] = jnp.broadcast_to(off, (N_EXPERTS, LANES)).astype(I32)
            return off + tot[c]

        off = jnp.zeros((N_EXPERTS, 1), F32)
        for c in range(nc):
            off = body(c, off)

    prefix(eq, pos_ref, None)
    sel = gt | (eq & (pos_ref[0] < need[None].astype(I32)))
    prefix(sel, pos_ref, off_ref)
    pos_ref[0] = jnp.where(sel, pos_ref[0], -1)
    gate_ref[0] = jnp.where(sel, aff, 0.0)


def _route(aff, tri, cap):
    b_, nc, _, _ = aff.shape
    blk = pl.BlockSpec((1, nc, N_EXPERTS, LANES), lambda b: (b, 0, 0, 0))
    return pl.pallas_call(
        functools.partial(_route_kernel, cap=cap),
        grid=(b_,),
        in_specs=[blk, pl.BlockSpec((LANES, LANES), lambda b: (0, 0))],
        out_specs=[blk, blk, blk],
        out_shape=[jax.ShapeDtypeStruct(aff.shape, I32), jax.ShapeDtypeStruct(aff.shape, F32),
                   jax.ShapeDtypeStruct(aff.shape, I32)],
        compiler_params=_cparams(("parallel",)),
        name="route",
    )(aff, tri)


def _slab_geometry(starts_ref, b, j, e):
    start = starts_ref[b, j, e]
    count = starts_ref[b, j + 1, e] - start
    base = (start // BF16_ROWS) * BF16_ROWS
    nslab = (start - base + count + SLAB - 1) // SLAB
    return base, jnp.where(count > 0, nslab, 0)


def _dispatch_kernel(starts_ref, x_ref, pos_ref, xe_ref):
    b = pl.program_id(0)
    eg = pl.program_id(1)
    j = pl.program_id(2)
    t = x_ref.shape[1]

    @pl.when(j == 0)
    def _zero():
        xe_ref[...] = jnp.zeros(xe_ref.shape, BF16)

    xt = x_ref[0]
    rid = lax.broadcasted_iota(I32, (SLAB, t), 0)
    geo, local = [], []
    for el in range(E_GROUP):
        e = eg * E_GROUP + el
        geo.append(_slab_geometry(starts_ref, b, j, e))
        prow = jnp.concatenate(
            [pos_ref[0, c, pl.ds(e, 1), :] for c in range(t // LANES)], axis=1)
        local.append(jnp.broadcast_to(prow - geo[el][0], (SLAB, t)))

    def add_rows(el, k, rows):
        dst = pl.ds(pl.multiple_of(geo[el][0] + k * SLAB, BF16_ROWS), SLAB)
        xe_ref[0, el, dst, :] = xe_ref[0, el, dst, :] + rows.astype(BF16)

    onehot = jnp.concatenate([(loc == rid).astype(BF16) for loc in local], axis=0)
    rows = _dot(onehot, xt)
    for el in range(E_GROUP):
        add_rows(el, 0, rows[el * SLAB:(el + 1) * SLAB])

    for el in range(E_GROUP):
        def body(k, carry, el=el):
            add_rows(el, k, _dot((local[el] == rid + k * SLAB).astype(BF16), xt))
            return carry

        lax.fori_loop(1, geo[el][1], body, 0)


def _dispatch(starts, x1b, pos, capp):
    b_, s_, d_ = x1b.shape
    t = min(T_TOK, s_)
    grid_spec = pltpu.PrefetchScalarGridSpec(
        num_scalar_prefetch=1,
        grid=(b_, N_EXPERTS // E_GROUP, s_ // t),
        in_specs=[
            pl.BlockSpec((1, t, d_), lambda b, g, j, *_: (b, j, 0)),
            pl.BlockSpec((1, t // LANES, N_EXPERTS, LANES), lambda b, g, j, *_: (b, j, 0, 0)),
        ],
        out_specs=pl.BlockSpec((1, E_GROUP, capp, d_), lambda b, g, j, *_: (b, g, 0, 0)),
    )
    return pl.pallas_call(
        _dispatch_kernel,
        grid_spec=grid_spec,
        out_shape=jax.ShapeDtypeStruct((b_, N_EXPERTS, capp, d_), BF16),
        compiler_params=_cparams(("parallel", "parallel", "arbitrary")),
        name="dispatch",
    )(starts, x1b, pos)


def _ffn_kernel(xe_ref, wg_ref, wu_ref, wd_ref, y_ref, acc_ref, *, cap):
    e = pl.program_id(0)
    fc = pl.program_id(1)
    b = pl.program_id(2)

    @pl.when((e == 0) & (fc == 0) & (b == 0))
    def _define_acc():
        acc_ref[...] = jnp.zeros(acc_ref.shape, F32)

    xe = xe_ref[0, 0]
    hg = _dot(xe, wg_ref[0].astype(BF16))
    hu = _dot(xe, wu_ref[0].astype(BF16))
    act = (hg * jax.nn.sigmoid(hg) * hu).astype(BF16)
    total = jnp.where(fc > 0, acc_ref[b], 0.0) + _dot(act, wd_ref[0].astype(BF16))
    acc_ref[b] = total
    y_ref[0, 0, 0:cap, :] = total.astype(BF16)
    y_ref[0, 0, cap:, :] = jnp.zeros((y_ref.shape[2] - cap, y_ref.shape[3]), BF16)


def _ffn(xe, w_gate, w_up, w_down, cap):
    b_, ne, capp, d_ = xe.shape
    f_ = w_gate.shape[2]
    fcw = min(FC, f_)
    nf = f_ // fcw
    return pl.pallas_call(
        functools.partial(_ffn_kernel, cap=cap),
        grid=(ne, nf, b_),
        in_specs=[
            pl.BlockSpec((1, 1, cap, d_), lambda e, f, b: (b, e, 0, 0)),
            pl.BlockSpec((1, d_, fcw), lambda e, f, b: (e, 0, f)),
            pl.BlockSpec((1, d_, fcw), lambda e, f, b: (e, 0, f)),
            pl.BlockSpec((1, fcw, d_), lambda e, f, b: (e, f, 0)),
        ],
        out_specs=pl.BlockSpec((1, 1, capp, d_), lambda e, f, b: (jnp.where(f == nf - 1, b, 0), e, 0, 0)),
        out_shape=jax.ShapeDtypeStruct((b_, ne, capp, d_), BF16),
        scratch_shapes=[pltpu.VMEM((b_, cap, d_), F32)],
        compiler_params=_cparams(("arbitrary", "arbitrary", "arbitrary")),
        name="ffn",
    )(xe, w_gate, w_up, w_down)


def _combine_kernel(starts_ref, y_hbm, pos_ref, gate_ref, x1_ref, p_ref, ex_ref, wpg_ref, wpp_ref,
                    g2_ref, b2_ref, g3_ref, b3_ref, o_ref, ybuf, xbuf, sem, xsem, moe_ref, *, alpha):
    b = pl.program_id(0)
    j = pl.program_id(1)
    nt = pl.num_programs(1)
    t = x1_ref.shape[1]
    slot = j % 2

    def first_slab_copy(jj, e, sl):
        base, _ = _slab_geometry(starts_ref, b, jj, e)
        src = y_hbm.at[b, e, pl.ds(pl.multiple_of(base, BF16_ROWS), SLAB), :]
        return pltpu.make_async_copy(src, ybuf.at[sl, pl.ds(e * SLAB, SLAB), :], sem.at[sl, e])

    @pl.when(j == 0)
    def _prime():
        for e in range(N_EXPERTS):
            first_slab_copy(j, e, slot).start()

    @pl.when(j + 1 < nt)
    def _prefetch_next_tile():
        for e in range(N_EXPERTS):
            first_slab_copy(j + 1, e, 1 - slot).start()

    pad = jnp.zeros((LANES - N_EXPERTS, LANES), F32)
    pos_t = jnp.concatenate(
        [jnp.concatenate([pos_ref[0, c].astype(F32), pad], axis=0).T for c in range(t // LANES)], axis=0)
    gate_t = jnp.concatenate(
        [jnp.concatenate([gate_ref[0, c], pad], axis=0).T for c in range(t // LANES)], axis=0)

    geo = [_slab_geometry(starts_ref, b, j, e) for e in range(N_EXPERTS)]
    lane_e = lax.broadcasted_iota(I32, (1, LANES), 1)
    slab_e = lax.broadcasted_iota(I32, (1, N_EXPERTS * SLAB), 1) // SLAB
    start_row = jnp.zeros((1, LANES), F32)
    shift_row = jnp.zeros((1, N_EXPERTS * SLAB), F32)
    for e, (base, _) in enumerate(geo):
        start = starts_ref[b, j, e]
        start_row = jnp.where(lane_e == e, start.astype(F32), start_row)
        shift_row = jnp.where(slab_e == e, (start - base).astype(F32), shift_row)
    rank = jnp.where(pos_t >= 0.0, pos_t - start_row, -512.0).astype(BF16)
    rank_x = _dot(rank, ex_ref[...])
    row_in_slab = (lax.broadcasted_iota(I32, (1, N_EXPERTS * SLAB), 1) % SLAB).astype(F32)
    match = rank_x == row_in_slab - shift_row
    g_hi, g_lo = _split_bf16(gate_t)
    oh_hi = jnp.where(match, _dot(g_hi, ex_ref[...]), 0.0).astype(BF16)
    oh_lo = jnp.where(match, _dot(g_lo, ex_ref[...]), 0.0).astype(BF16)

    for e in range(N_EXPERTS):
        first_slab_copy(j, e, slot).wait()
    ycat = ybuf[slot]
    moe_ref[...] = _dot(oh_hi, ycat) + _dot(oh_lo, ycat)

    cid = lax.broadcasted_iota(I32, (t, SLAB), 1).astype(F32)
    for e, (base, nslab) in enumerate(geo):
        def body(k, carry, e=e, base=base):
            src = y_hbm.at[b, e, pl.ds(pl.multiple_of(base + k * SLAB, BF16_ROWS), SLAB), :]
            cp = pltpu.make_async_copy(src, xbuf, xsem)
            cp.start()
            local = jnp.broadcast_to(pos_t[:, e:e + 1] - (base + k * SLAB).astype(F32), (t, SLAB))
            onehot = (local == cid).astype(BF16)
            cp.wait()
            moe_ref[...] = moe_ref[...] + gate_t[:, e:e + 1] * _dot(onehot, xbuf[...])
            return carry

        lax.fori_loop(1, nslab, body, 0)

    x2 = _layer_norm(alpha * x1_ref[0] + moe_ref[...], g2_ref[...], b2_ref[...])
    gate = jax.nn.sigmoid(_dot(x2.astype(BF16), wpg_ref[...]))
    ple = _dot(p_ref[0].astype(BF16), wpp_ref[...]) * gate
    o_ref[0] = _layer_norm(alpha * x2 + ple, g3_ref[...], b3_ref[...])


def _combine(starts, y, pos, gate, x1, p, wpg, wpp, g2, b2, g3, b3, alpha):
    b_, s_, d_ = x1.shape
    t = min(T_TOK, s_)
    pd = p.shape[2]
    ex = (jnp.arange(LANES)[:, None] == jnp.arange(N_EXPERTS * SLAB)[None, :] // SLAB).astype(BF16)
    full = lambda shape: pl.BlockSpec(shape, lambda b, j, *_: (0,) * len(shape))
    tile4 = pl.BlockSpec((1, t // LANES, N_EXPERTS, LANES), lambda b, j, *_: (b, j, 0, 0))
    grid_spec = pltpu.PrefetchScalarGridSpec(
        num_scalar_prefetch=1,
        grid=(b_, s_ // t),
        in_specs=[
            pl.BlockSpec(memory_space=pl.ANY),
            tile4, tile4,
            pl.BlockSpec((1, t, d_), lambda b, j, *_: (b, j, 0)),
            pl.BlockSpec((1, t, pd), lambda b, j, *_: (b, j, 0)),
            full((LANES, N_EXPERTS * SLAB)),
            full((d_, d_)), full((pd, d_)),
            full((1, d_)), full((1, d_)), full((1, d_)), full((1, d_)),
        ],
        out_specs=pl.BlockSpec((1, t, d_), lambda b, j, *_: (b, j, 0)),
        scratch_shapes=[pltpu.VMEM((2, N_EXPERTS * SLAB, d_), BF16),
                        pltpu.VMEM((SLAB, d_), BF16),
                        pltpu.SemaphoreType.DMA((2, N_EXPERTS)),
                        pltpu.SemaphoreType.DMA(()),
                        pltpu.VMEM((t, d_), F32)],
    )
    return pl.pallas_call(
        functools.partial(_combine_kernel, alpha=alpha),
        grid_spec=grid_spec,
        out_shape=jax.ShapeDtypeStruct((b_, s_, d_), F32),
        compiler_params=_cparams(("parallel", "arbitrary")),
        name="combine",
    )(starts, y, pos, gate, x1, p, ex, wpg, wpp, g2, b2, g3, b3)


def _rope_tables(s_):
    half = HEAD_DIM // 2
    inv = ROPE_THETA ** (-jnp.arange(0, half, 2, dtype=F32) / half)
    t = jnp.arange(s_, dtype=jnp.int32)
    row = (t // GRID_W).astype(F32)[:, None] * inv[None, :]
    col = (t % GRID_W).astype(F32)[:, None] * inv[None, :]
    cos = jnp.concatenate([jnp.cos(row), jnp.cos(row), jnp.cos(col), jnp.cos(col)], axis=1)
    sin = jnp.concatenate([-jnp.sin(row), jnp.sin(row), -jnp.sin(col), jnp.sin(col)], axis=1)
    reps = LANES // HEAD_DIM
    return jnp.tile(cos, (1, reps)), jnp.tile(sin, (1, reps))


def _t5_bucket(rel):
    half = NUM_BUCKETS // 2
    max_exact = half // 2
    ret = (rel > 0).astype(jnp.int32) * half
    n = jnp.abs(rel)
    nf = jnp.maximum(n, 1).astype(F32)
    large = max_exact + (jnp.log(nf / max_exact) / math.log(MAX_DISTANCE / max_exact)
                         * (half - max_exact)).astype(jnp.int32)
    large = jnp.minimum(large, half - 1)
    return ret + jnp.where(n < max_exact, n, large)


def _bucket_thresholds():
    half = NUM_BUCKETS // 2
    n = jnp.arange(0, MAX_DISTANCE + 1, dtype=jnp.int32)
    bk = _t5_bucket(-n)
    j = jnp.arange(half, dtype=jnp.int32)
    return jnp.sum((bk[None, :] < j[:, None]).astype(jnp.int32), axis=1)


def _group_mean_matrix(width):
    g = jnp.arange(width, dtype=jnp.int32) // HEAD_DIM
    return ((g[:, None] == g[None, :]).astype(F32) / HEAD_DIM).astype(BF16)


def kernel(x, p, w_in, w_out, a_q_norm, a_k_norm, b_lambda_q1, b_lambda_k1, b_lambda_q2, b_lambda_k2,
           b_subln, rel_bias, ln1_g, ln1_b, w_router, w_gate, w_up, w_down, ln2_g, ln2_b,
           w_ple_gate, w_ple_proj, ln3_g, ln3_b):
    b_, s_, d_ = x.shape
    depth = w_in.shape[0]
    alpha = (2 * depth) ** 0.25
    cap = EC_CAPACITY_FACTOR * s_ // N_EXPERTS
    t_tok = min(T_TOK, s_)
    capp = cap + SLAB
    assert s_ % GRID_W == 0 and s_ % LANES == 0 and cap % BF16_ROWS == 0

    cs, sn = _rope_tables(s_)
    thr = _bucket_thresholds()
    bias_vals = (rel_bias.astype(F32) * LOG2E).T
    gmq, gmk = _group_mean_matrix(A_Q), _group_mean_matrix(A_KV)
    tri = (jnp.arange(LANES)[:, None] <= jnp.arange(LANES)[None, :]).astype(BF16)
    row = lambda v: v.astype(F32).reshape(1, -1)

    for i in range(depth):
        lam_init = 0.8 - 0.6 * math.exp(-0.3 * i)
        qa, ka, vat, qb, kb, vbt, qb_sq, kb_sq = _inproj(
            x, w_in[i].astype(BF16), cs, sn,
            jnp.tile(row(a_q_norm[i]), (1, A_HEADS)), jnp.tile(row(a_k_norm[i]), (1, A_KV_HEADS)), gmq, gmk)
        bound_a = (HEAD_DIM * ATTN_SCALE * LOG2E * 1.01
                   * jnp.max(jnp.abs(a_q_norm[i].astype(F32))) * jnp.max(jnp.abs(a_k_norm[i].astype(F32))))
        bound_b = 1.03 * jnp.sqrt(jnp.max(qb_sq) * jnp.max(kb_sq)) + jnp.max(jnp.abs(bias_vals))
        score_bytes = 2 * s_ * 4 * max(min(TQ_A_BOUNDED, s_) // A_QUERY_SPLIT, min(T_B, s_) // B_QUERY_SPLIT)
        fits = score_bytes <= VMEM_LIMIT // 2
        oa = lax.cond(fits & (bound_a <= EXP_RANGE),
                      lambda *a: _attn_a(*a, bounded=True), lambda *a: _attn_a(*a, bounded=False), qa, ka, vat)
        b_args = (qb, kb, vbt, thr, bias_vals, row(b_lambda_q1[i]), row(b_lambda_k1[i]),
                  row(b_lambda_q2[i]), row(b_lambda_k2[i]), b_subln[i].astype(F32).reshape(-1, 1))
        ob = lax.cond(fits & (bound_b <= EXP_RANGE),
                      lambda *a: _attn_b(*a, lam_init, bounded=True),
                      lambda *a: _attn_b(*a, lam_init, bounded=False), *b_args)
        wr = jnp.pad(w_router[i].astype(F32), ((0, 0), (0, LANES - N_EXPERTS)))
        rh = wr.astype(BF16)
        rl = (wr - rh.astype(F32)).astype(BF16)
        wo = w_out[i].astype(BF16)
        x1, x1b, aff = _outproj(oa, ob, x, wo[:A_Q], wo[A_Q:], row(ln1_g[i]), row(ln1_b[i]),
                                jnp.concatenate([rh, rl], axis=1), alpha)
        pos, gate, off = _route(aff, tri, cap)
        starts = jnp.concatenate(
            [off[:, ::t_tok // LANES, :, 0], jnp.full((b_, 1, N_EXPERTS), cap, jnp.int32)], axis=1)
        xe = _dispatch(starts, x1b, pos, capp)
        y = _ffn(xe, w_gate[i], w_up[i], w_down[i], cap)
        x = _combine(starts, y, pos, gate, x1, p[i], w_ple_gate[i].astype(BF16), w_ple_proj[i].astype(BF16),
                     row(ln2_g[i]), row(ln2_b[i]), row(ln3_g[i]), row(ln3_b[i]), alpha)
    return x
```

```python
import functools
import math

import jax
import jax.numpy as jnp
from jax import lax
from jax.experimental import pallas as pl
from jax.experimental.pallas import tpu as pltpu

F32 = jnp.float32
BF16 = jnp.bfloat16
I32 = jnp.int32

HEAD_DIM = 64
A_HEADS = 8
A_KV_HEADS = 2
A_REP = A_HEADS // A_KV_HEADS
B_HEADS = 4
B_VDIM = 2 * HEAD_DIM
A_Q = A_HEADS * HEAD_DIM
A_KV = A_KV_HEADS * HEAD_DIM
B_QK = B_HEADS * 2 * HEAD_DIM
B_WIDTH = B_HEADS * B_VDIM
ATTN_SCALE = HEAD_DIM ** -0.5
GRID_W = 64
ROPE_THETA = 10000.0
NUM_BUCKETS = 32
MAX_DISTANCE = 128
N_EXPERTS = 16
EC_CAPACITY_FACTOR = 2
LN_EPS = 1e-5
QK_EPS = 1e-6
LOG2E = math.log2(math.e)

LANES = 128
BF16_ROWS = 16
ONES_ROWS = BF16_ROWS
VMEM_LIMIT = 56 * 1024 * 1024

TM_PROJ = 1024
TQ_A = 256
TK_A = 512
T_B = 512
TQ_A_BOUNDED = 512
A_QUERY_SPLIT = 2
B_SUBTILES_BOUNDED = 2
B_QUERY_SPLIT = 2
CHUNKS_PER_TRIP = 4
BIAS_TILES = 5
T_TOK = 256
SLAB = 64
E_GROUP = 8
FC = 512
NEG_BIG = -1e30
EXP_RANGE = 40.0


def _cparams(sem):
    return pltpu.CompilerParams(dimension_semantics=sem, vmem_limit_bytes=VMEM_LIMIT)


def _dot(a, b):
    return jnp.dot(a, b, preferred_element_type=F32)


def _dot_nt(a, b):
    return lax.dot_general(a, b, (((1,), (1,)), ((), ())), preferred_element_type=F32)


def _layer_norm(y, g, b):
    mu = jnp.mean(y, axis=-1, keepdims=True)
    yc = y - mu
    var = jnp.mean(yc * yc, axis=-1, keepdims=True)
    return yc * lax.rsqrt(var + LN_EPS) * g + b


def _split_bf16(v):
    hi = v.astype(BF16)
    lo = (v - hi.astype(F32)).astype(BF16)
    return hi, lo


def _inproj_kernel(x_ref, w_ref, wvt_ref, cs_ref, sn_ref, gq_ref, gk_ref, gmq_ref, gmk_ref, gsum_ref,
                   qa_ref, ka_ref, vat_ref, qb_ref, kb_ref, vbt_ref, qn_ref, kn_ref):
    tm = x_ref.shape[1]
    xb = x_ref[0].astype(BF16)
    lane = lax.broadcasted_iota(I32, (tm, LANES), 1)
    lo_half = lane < HEAD_DIM
    cs = cs_ref[...]
    sn = sn_ref[...]
    qscale = ATTN_SCALE * LOG2E
    ones_rows = (lax.broadcasted_iota(I32, (ONES_ROWS, tm), 0) == 0).astype(BF16)

    def group_rms(v, gm_ref, gain):
        hi, lo = _split_bf16(v * v)
        ms = _dot(hi, gm_ref[...]) + _dot(lo, gm_ref[...])
        return v * lax.rsqrt(ms + QK_EPS) * gain

    def rope(v, reps):
        width = v.shape[1]
        lane_w = lax.broadcasted_iota(I32, v.shape, 1)
        first = (lane_w % (HEAD_DIM // 2)) < (HEAD_DIM // 4)
        rot = jnp.where(first, pltpu.roll(v, width - HEAD_DIM // 4, 1), pltpu.roll(v, HEAD_DIM // 4, 1))
        c = jnp.concatenate([cs] * reps, axis=1) if reps > 1 else cs
        s = jnp.concatenate([sn] * reps, axis=1) if reps > 1 else sn
        return v * c + rot * s

    qa = _dot(xb, w_ref[:, 0:A_Q])
    qa = rope(group_rms(qa, gmq_ref, gq_ref[...]), A_Q // LANES) * qscale
    for c in range(A_Q // LANES):
        chunk = qa[:, c * LANES:(c + 1) * LANES]
        swapped = pltpu.roll(chunk, HEAD_DIM, 1)
        if (2 * c) // A_REP == 0:
            even, odd = jnp.where(lo_half, chunk, 0.0), jnp.where(lo_half, swapped, 0.0)
        else:
            even, odd = jnp.where(lo_half, 0.0, swapped), jnp.where(lo_half, 0.0, chunk)
        qa_ref[0, 2 * c] = even.astype(BF16)
        qa_ref[0, 2 * c + 1] = odd.astype(BF16)

    ka = _dot(xb, w_ref[:, A_Q:A_Q + A_KV])
    ka_ref[0] = rope(group_rms(ka, gmk_ref, gk_ref[...]), 1).astype(BF16)
    v_t = _dot_nt(wvt_ref[...], xb).astype(BF16)
    for g in range(A_KV_HEADS):
        vat_ref[0, g] = jnp.concatenate([v_t[g * HEAD_DIM:(g + 1) * HEAD_DIM], ones_rows], axis=0)

    o = A_Q + 2 * A_KV
    qb = _dot(xb, w_ref[:, o:o + B_QK]) * qscale
    kb = _dot(xb, w_ref[:, o + B_QK:o + 2 * B_QK])
    qn_ref[0, 0] = jnp.max(_dot((qb * qb).astype(BF16), gsum_ref[...]), axis=0, keepdims=True)
    kn_ref[0, 0] = jnp.max(_dot((kb * kb).astype(BF16), gsum_ref[...]), axis=0, keepdims=True)
    for h in range(B_HEADS):
        chunk = qb[:, h * LANES:(h + 1) * LANES]
        qb_ref[0, h, 0] = jnp.where(lo_half, chunk, 0.0).astype(BF16)
        qb_ref[0, h, 1] = jnp.where(lo_half, 0.0, chunk).astype(BF16)
        kb_ref[0, h] = kb[:, h * LANES:(h + 1) * LANES].astype(BF16)
        vbt_ref[0, h] = jnp.concatenate(
            [v_t[A_KV + h * B_VDIM:A_KV + (h + 1) * B_VDIM], ones_rows], axis=0)


def _inproj(x, w_in_bf, cs, sn, gq, gk, gmq, gmk):
    b_, s_, d_ = x.shape
    tm = min(TM_PROJ, s_)
    nst = s_ // tm
    in_w = w_in_bf.shape[1]
    vb0 = A_Q + 2 * A_KV + 2 * B_QK
    wv_t = jnp.concatenate([w_in_bf[:, A_Q + A_KV:A_Q + 2 * A_KV], w_in_bf[:, vb0:vb0 + B_WIDTH]], axis=1).T
    gsum = (jnp.arange(B_QK)[:, None] // HEAD_DIM == jnp.arange(LANES)[None, :]).astype(BF16)
    full = lambda shape: pl.BlockSpec(shape, lambda b, i: (0,) * len(shape))
    return pl.pallas_call(
        _inproj_kernel,
        grid=(b_, nst),
        in_specs=[
            pl.BlockSpec((1, tm, d_), lambda b, i: (b, i, 0)),
            full((d_, in_w)),
            full((A_KV + B_WIDTH, d_)),
            pl.BlockSpec((tm, LANES), lambda b, i: (i, 0)),
            pl.BlockSpec((tm, LANES), lambda b, i: (i, 0)),
            full((1, A_Q)), full((1, A_KV)), full((A_Q, A_Q)), full((A_KV, A_KV)), full((B_QK, LANES)),
        ],
        out_specs=[
            pl.BlockSpec((1, A_HEADS, tm, LANES), lambda b, i: (b, 0, i, 0)),
            pl.BlockSpec((1, tm, LANES), lambda b, i: (b, i, 0)),
            pl.BlockSpec((1, A_KV_HEADS, HEAD_DIM + ONES_ROWS, tm), lambda b, i: (b, 0, 0, i)),
            pl.BlockSpec((1, B_HEADS, 2, tm, LANES), lambda b, i: (b, 0, 0, i, 0)),
            pl.BlockSpec((1, B_HEADS, tm, LANES), lambda b, i: (b, 0, i, 0)),
            pl.BlockSpec((1, B_HEADS, B_VDIM + ONES_ROWS, tm), lambda b, i: (b, 0, 0, i)),
            pl.BlockSpec((1, 1, 1, LANES), lambda b, i: (b, i, 0, 0)),
            pl.BlockSpec((1, 1, 1, LANES), lambda b, i: (b, i, 0, 0)),
        ],
        out_shape=[
            jax.ShapeDtypeStruct((b_, A_HEADS, s_, LANES), BF16),
            jax.ShapeDtypeStruct((b_, s_, LANES), BF16),
            jax.ShapeDtypeStruct((b_, A_KV_HEADS, HEAD_DIM + ONES_ROWS, s_), BF16),
            jax.ShapeDtypeStruct((b_, B_HEADS, 2, s_, LANES), BF16),
            jax.ShapeDtypeStruct((b_, B_HEADS, s_, LANES), BF16),
            jax.ShapeDtypeStruct((b_, B_HEADS, B_VDIM + ONES_ROWS, s_), BF16),
            jax.ShapeDtypeStruct((b_, nst, 1, LANES), F32),
            jax.ShapeDtypeStruct((b_, nst, 1, LANES), F32),
        ],
        compiler_params=_cparams(("parallel", "parallel")),
        name="inproj",
    )(x, w_in_bf, wv_t, cs, sn, gq, gk, gmq, gmk, gsum)


def _softmax_step(s, smax, vt_chunk, shift, m_ref, acc_ref):
    m_prev = m_ref[...]
    m_cur = jnp.maximum(m_prev, smax)
    alpha = jnp.exp2(m_prev - m_cur)
    p = jnp.exp2(s - (m_cur - shift)).astype(BF16)
    acc_ref[...] = alpha * acc_ref[...] + _dot(vt_chunk, p)
    m_ref[...] = m_cur


def _init_stats(m_ref, acc_ref):
    m_ref[...] = jnp.full(m_ref.shape, NEG_BIG, F32)
    acc_ref[...] = jnp.zeros(acc_ref.shape, F32)


def _pipelined_chunks(nk, scores, consume, s_ref, smax_ref):
    assert nk % 2 == 0
    group = CHUNKS_PER_TRIP if nk % CHUNKS_PER_TRIP == 0 else 2

    def produce(j, slot):
        s, shift = scores(j)
        s_ref[slot] = s
        smax_ref[slot] = jnp.max(s, axis=0, keepdims=True) + shift

    def step(j, slot, prefetch):
        if prefetch:
            produce(j + 1, 1 - slot)
        consume(j, s_ref[slot], smax_ref[slot])

    def trip(i, carry):
        for u in range(group):
            step(group * i + u, u % 2, True)
        return carry

    produce(0, 0)
    lax.fori_loop(0, nk // group - 1, trip, 0)
    for u in range(group):
        step(nk - group + u, u % 2, u < group - 1)


def _attn_a_kernel(q_ref, k_ref, vt_ref, o_ref, m_ref, acc_ref, s_ref, smax_ref, *, tk):
    tq = q_ref.shape[2]
    cols = A_REP * tq
    nk = k_ref.shape[1] // tk
    q = q_ref[0].reshape(cols, LANES)
    _init_stats(m_ref, acc_ref)

    def scores(j):
        return _dot_nt(k_ref[0, pl.ds(pl.multiple_of(j * tk, tk), tk), :], q), 0.0

    def consume(j, s, smax):
        vt = vt_ref[0, 0, :, pl.ds(pl.multiple_of(j * tk, tk), tk)]
        _softmax_step(s, smax, vt, 0.0, m_ref, acc_ref)

    _pipelined_chunks(nk, scores, consume, s_ref, smax_ref)
    o = acc_ref[0:HEAD_DIM, :] / acc_ref[HEAD_DIM:HEAD_DIM + 1, :]
    o = jnp.concatenate([o[:, r * tq:(r + 1) * tq] for r in range(A_REP)], axis=0)
    o_ref[0] = o.T.astype(BF16)


def _bounded_softmax_groups(ngroups, scores, values_t, bias=None):
    vdim = values_t.shape[0] - ONES_ROWS
    s_next = scores(0)
    out = []
    for g in range(ngroups):
        s = s_next
        if g + 1 < ngroups:
            s_next = scores(g + 1)
        if bias is not None:
            s = bias(g, s)
        acc = _dot(values_t, jnp.exp2(s).astype(BF16))
        out.append(acc[0:vdim] / acc[vdim:vdim + 1])
    return out


def _attn_a_bounded_kernel(q_ref, k_ref, vt_ref, o_ref):
    tq = q_ref.shape[2]
    gw = tq // A_QUERY_SPLIT
    qs = [q_ref[0, r, part * gw:(part + 1) * gw, :] for r in range(A_REP) for part in range(A_QUERY_SPLIT)]
    outs = _bounded_softmax_groups(len(qs), lambda g: _dot_nt(k_ref[0], qs[g]), vt_ref[0, 0])
    heads = [jnp.concatenate(outs[r * A_QUERY_SPLIT:(r + 1) * A_QUERY_SPLIT], axis=1) for r in range(A_REP)]
    o_ref[0] = jnp.concatenate(heads, axis=0).T.astype(BF16)


def _attn_a(qa, ka, vat, bounded):
    b_, _, s_, _ = qa.shape
    tq = min(TQ_A_BOUNDED if bounded else TQ_A, s_)
    tk = min(TK_A, s_)
    cols = A_REP * tq
    vrows = vat.shape[2]
    if bounded:
        body = _attn_a_bounded_kernel
        scratch = []
    else:
        body = functools.partial(_attn_a_kernel, tk=tk)
        scratch = [pltpu.VMEM((1, cols), F32), pltpu.VMEM((vrows, cols), F32),
                   pltpu.VMEM((2, tk, cols), F32), pltpu.VMEM((2, 1, cols), F32)]
    return pl.pallas_call(
        body,
        grid=(b_, A_KV_HEADS, s_ // tq),
        in_specs=[
            pl.BlockSpec((1, A_REP, tq, LANES), lambda b, g, i: (b, g, i, 0)),
            pl.BlockSpec((1, s_, LANES), lambda b, g, i: (b, 0, 0)),
            pl.BlockSpec((1, 1, vrows, s_), lambda b, g, i: (b, g, 0, 0)),
        ],
        out_specs=pl.BlockSpec((1, tq, A_REP * HEAD_DIM), lambda b, g, i: (b, i, g)),
        out_shape=jax.ShapeDtypeStruct((b_, s_, A_Q), BF16),
        scratch_shapes=scratch,
        compiler_params=_cparams(("parallel", "parallel", "arbitrary")),
        name="attn_a_bounded" if bounded else "attn_a",
    )(qa, ka, vat)


def _build_bias_tiles(thr_ref, val_ref, bias_ref, h, t):
    half = NUM_BUCKETS // 2
    reach = (BIAS_TILES - 1) // 2
    key = lax.broadcasted_iota(I32, (t, t), 0)
    qry = lax.broadcasted_iota(I32, (t, t), 1)
    for d in range(-reach, reach + 1):
        rel = key - qry + d * t
        n = jnp.abs(rel)
        neg = jnp.full((t, t), val_ref[h, 0], F32)
        pos = jnp.full((t, t), val_ref[h, half], F32)
        for j in range(1, half):
            ge = n >= thr_ref[j]
            neg = jnp.where(ge, val_ref[h, j], neg)
            pos = jnp.where(ge, val_ref[h, half + j], pos)
        bias_ref[d + reach] = jnp.where(rel > 0, pos, neg)


def _chunk_order(qi, nk):
    reach = (BIAS_TILES - 1) // 2
    n_near = min(nk, 2 * reach - 1)
    near0 = jnp.clip(qi - (n_near // 2), 0, nk - n_near)

    def chunk_of(v):
        if isinstance(v, int) and v >= nk - n_near:
            return near0 + (v - (nk - n_near)), True
        return v + jnp.where(v >= near0, n_near, 0), False

    return n_near, chunk_of


def _diff_attn_finish(o, t, lq1_ref, lk1_ref, lq2_ref, lk2_ref, sub_ref, o_ref, lam_init):
    lam = (jnp.exp(jnp.sum(lq1_ref[...] * lk1_ref[...], axis=1, keepdims=True))
           - jnp.exp(jnp.sum(lq2_ref[...] * lk2_ref[...], axis=1, keepdims=True)) + lam_init)
    o = o[:, 0:t] - lam * o[:, t:2 * t]
    ms = jnp.mean(o * o, axis=0, keepdims=True)
    o = o * lax.rsqrt(ms + LN_EPS) * sub_ref[...] * (1.0 - lam_init)
    o_ref[0] = o.T.astype(BF16)


def _attn_b_bounded_kernel(thr_ref, val_ref, q_ref, k_ref, vt_ref, lq1_ref, lk1_ref, lq2_ref, lk2_ref,
                           sub_ref, o_ref, bias_ref, *, lam_init):
    h = pl.program_id(1)
    step = pl.program_id(2)
    t = bias_ref.shape[1]
    nsub = q_ref.shape[3] // t
    nk = k_ref.shape[2] // t
    reach = (BIAS_TILES - 1) // 2
    gw = t // B_QUERY_SPLIT
    per_sub = 2 * B_QUERY_SPLIT

    @pl.when(step == 0)
    def _bias():
        _build_bias_tiles(thr_ref, val_ref, bias_ref, h, t)

    def group(g):
        sub, stream, part = g // per_sub, (g % per_sub) // B_QUERY_SPLIT, g % B_QUERY_SPLIT
        return sub, stream, part

    def queries(g):
        sub, stream, part = group(g)
        lo = sub * t + part * gw
        return q_ref[0, 0, stream, lo:lo + gw, :]

    def add_bias(g, s):
        sub, _, part = group(g)
        qi = step * nsub + sub
        tiles = [bias_ref[jnp.clip(j - qi, -reach, reach) + reach][:, part * gw:(part + 1) * gw]
                 for j in range(nk)]
        return jnp.concatenate([s[j * t:(j + 1) * t] + tiles[j] for j in range(nk)], axis=0)

    qs = [queries(g) for g in range(nsub * per_sub)]
    outs = _bounded_softmax_groups(len(qs), lambda g: _dot_nt(k_ref[0, 0], qs[g]), vt_ref[0, 0], add_bias)
    for sub in range(nsub):
        _diff_attn_finish(jnp.concatenate(outs[sub * per_sub:(sub + 1) * per_sub], axis=1), t,
                          lq1_ref, lk1_ref, lq2_ref, lk2_ref, sub_ref, o_ref.at[:, sub * t:(sub + 1) * t, :],
                          lam_init)


def _attn_b_kernel(thr_ref, val_ref, q_ref, k_ref, vt_ref, lq1_ref, lk1_ref, lq2_ref, lk2_ref, sub_ref,
                   o_ref, bias_ref, m_ref, acc_ref, s_ref, smax_ref, *, lam_init):
    h = pl.program_id(1)
    qi = pl.program_id(2)
    t = q_ref.shape[3]
    nk = k_ref.shape[2] // t
    half = NUM_BUCKETS // 2
    reach = (BIAS_TILES - 1) // 2

    @pl.when(qi == 0)
    def _bias():
        _build_bias_tiles(thr_ref, val_ref, bias_ref, h, t)

    q = q_ref[0, 0].reshape(2 * t, LANES)
    _init_stats(m_ref, acc_ref)

    n_near, chunk_of = _chunk_order(qi, nk)
    assert nk <= CHUNKS_PER_TRIP or (nk % CHUNKS_PER_TRIP == 0 and CHUNKS_PER_TRIP > n_near)

    def far_shift(j):
        return jnp.where(j < qi, val_ref[h, half - 1], val_ref[h, NUM_BUCKETS - 1])

    def scores(v):
        j, near = chunk_of(v)
        s = _dot_nt(k_ref[0, 0, pl.ds(pl.multiple_of(j * t, t), t), :], q)
        if near:
            bias = bias_ref[jnp.clip(j - qi, -reach, reach) + reach]
            return s + jnp.concatenate([bias, bias], axis=1), 0.0
        return s, far_shift(j)

    def consume(v, s, smax):
        j, near = chunk_of(v)
        vt = vt_ref[0, 0, :, pl.ds(pl.multiple_of(j * t, t), t)]
        _softmax_step(s, smax, vt, 0.0 if near else far_shift(j), m_ref, acc_ref)

    _pipelined_chunks(nk, scores, consume, s_ref, smax_ref)
    _diff_attn_finish(acc_ref[0:B_VDIM, :] / acc_ref[B_VDIM:B_VDIM + 1, :], t,
                      lq1_ref, lk1_ref, lq2_ref, lk2_ref, sub_ref, o_ref, lam_init)


def _attn_b(qb, kb, vbt, thr, vals, lq1, lk1, lq2, lk2, subln_col, lam_init, bounded):
    b_, _, _, s_, _ = qb.shape
    t = min(T_B, s_)
    assert t >= MAX_DISTANCE
    vrows = vbt.shape[2]
    tq = t
    if bounded:
        body = functools.partial(_attn_b_bounded_kernel, lam_init=lam_init)
        scratch = [pltpu.VMEM((BIAS_TILES, t, t), F32)]
        tq = t * B_SUBTILES_BOUNDED if s_ % (t * B_SUBTILES_BOUNDED) == 0 else t
    else:
        body = functools.partial(_attn_b_kernel, lam_init=lam_init)
        scratch = [pltpu.VMEM((BIAS_TILES, t, t), F32), pltpu.VMEM((1, 2 * t), F32),
                   pltpu.VMEM((vrows, 2 * t), F32),
                   pltpu.VMEM((2, t, 2 * t), F32), pltpu.VMEM((2, 1, 2 * t), F32)]
    vec = lambda n: pl.BlockSpec((1, n), lambda b, h, i, *_: (0, 0))
    grid_spec = pltpu.PrefetchScalarGridSpec(
        num_scalar_prefetch=2,
        grid=(b_, B_HEADS, s_ // tq),
        in_specs=[
            pl.BlockSpec((1, 1, 2, tq, LANES), lambda b, h, i, *_: (b, h, 0, i, 0)),
            pl.BlockSpec((1, 1, s_, LANES), lambda b, h, i, *_: (b, h, 0, 0)),
            pl.BlockSpec((1, 1, vrows, s_), lambda b, h, i, *_: (b, h, 0, 0)),
            vec(HEAD_DIM), vec(HEAD_DIM), vec(HEAD_DIM), vec(HEAD_DIM),
            pl.BlockSpec((B_VDIM, 1), lambda b, h, i, *_: (0, 0)),
        ],
        out_specs=pl.BlockSpec((1, tq, B_VDIM), lambda b, h, i, *_: (b, i, h)),
        scratch_shapes=scratch,
    )
    return pl.pallas_call(
        body,
        grid_spec=grid_spec,
        out_shape=jax.ShapeDtypeStruct((b_, s_, B_WIDTH), BF16),
        compiler_params=_cparams(("parallel", "parallel", "arbitrary")),
        name="attn_b_bounded" if bounded else "attn_b",
    )(thr, vals, qb, kb, vbt, lq1, lk1, lq2, lk2, subln_col)


def _outproj_kernel(oa_ref, ob_ref, x_ref, wa_ref, wb_ref, g_ref, b_ref, r_ref,
                    x1_ref, x1b_ref, aff_ref, *, alpha):
    tm = x_ref.shape[1]
    mix = _dot(oa_ref[0], wa_ref[...]) + _dot(ob_ref[0], wb_ref[...])
    x1 = _layer_norm(alpha * x_ref[0] + mix, g_ref[...], b_ref[...])
    x1_ref[0] = x1
    x1b_ref[0] = x1.astype(BF16)
    hi, lo = _split_bf16(x1)
    parts = _dot(hi, r_ref[...]) + _dot(lo, r_ref[...])
    logits = parts[:, 0:LANES] + parts[:, LANES:2 * LANES]
    lane = lax.broadcasted_iota(I32, (tm, LANES), 1)
    logits = jnp.where(lane < N_EXPERTS, logits, NEG_BIG)
    e = jnp.exp(logits - jnp.max(logits, axis=1, keepdims=True))
    aff = e / jnp.sum(e, axis=1, keepdims=True)
    aff_t = aff.T
    for c in range(tm // LANES):
        aff_ref[0, c] = aff_t[0:N_EXPERTS, c * LANES:(c + 1) * LANES]


def _outproj(oa, ob, x, wo_a, wo_b, g, b, r_hi_lo, alpha):
    b_, s_, d_ = x.shape
    tm = min(TM_PROJ, s_)
    nch = tm // LANES
    full = lambda shape: pl.BlockSpec(shape, lambda b, i: (0,) * len(shape))
    return pl.pallas_call(
        functools.partial(_outproj_kernel, alpha=alpha),
        grid=(b_, s_ // tm),
        in_specs=[
            pl.BlockSpec((1, tm, A_Q), lambda b, i: (b, i, 0)),
            pl.BlockSpec((1, tm, B_WIDTH), lambda b, i: (b, i, 0)),
            pl.BlockSpec((1, tm, d_), lambda b, i: (b, i, 0)),
            full((A_Q, d_)), full((B_WIDTH, d_)), full((1, d_)), full((1, d_)),
            full((d_, 2 * LANES)),
        ],
        out_specs=[
            pl.BlockSpec((1, tm, d_), lambda b, i: (b, i, 0)),
            pl.BlockSpec((1, tm, d_), lambda b, i: (b, i, 0)),
            pl.BlockSpec((1, nch, N_EXPERTS, LANES), lambda b, i: (b, i, 0, 0)),
        ],
        out_shape=[
            jax.ShapeDtypeStruct((b_, s_, d_), F32),
            jax.ShapeDtypeStruct((b_, s_, d_), BF16),
            jax.ShapeDtypeStruct((b_, s_ // LANES, N_EXPERTS, LANES), F32),
        ],
        compiler_params=_cparams(("parallel", "parallel")),
        name="outproj",
    )(oa, ob, x, wo_a, wo_b, g, b, r_hi_lo)


def _route_kernel(aff_ref, tri_ref, pos_ref, gate_ref, off_ref, *, cap):
    nc = aff_ref.shape[1]
    aff = aff_ref[0]
    bits = pltpu.bitcast(aff, I32)

    def count(mask):
        per_lane = jnp.sum(mask.astype(F32), axis=0)
        return jnp.sum(per_lane, axis=1, keepdims=True)

    thr = jnp.zeros((N_EXPERTS, 1), I32)
    for bit in range(30, -1, -1):
        cand = thr | (1 << bit)
        thr = jnp.where(count(bits >= cand[None]) >= cap, cand, thr)

    gt = bits > thr[None]
    eq = bits == thr[None]
    need = cap - count(gt)

    def prefix(mask, out_ref, extra_ref):
        mb = mask.astype(BF16).reshape(nc * N_EXPERTS, LANES)
        incl = _dot(mb, tri_ref[...]).reshape(nc, N_EXPERTS, LANES)
        excl = incl - mask.astype(F32)
        tot = incl[:, :, LANES - 1:LANES]

        def body(c, off):
            out_ref[0, c] = (excl[c] + off).astype(I32)
            if extra_ref is not None:
                extra_ref[0, c] = jnp.broadcast_to(off, (N_EXPERTS, LANES)).astype(I32)
            return off + tot[c]

        off = jnp.zeros((N_EXPERTS, 1), F32)
        for c in range(nc):
            off = body(c, off)

    prefix(eq, pos_ref, None)
    sel = gt | (eq & (pos_ref[0] < need[None].astype(I32)))
    prefix(sel, pos_ref, off_ref)
    pos_ref[0] = jnp.where(sel, pos_ref[0], -1)
    gate_ref[0] = jnp.where(sel, aff, 0.0)


def _route(aff, tri, cap):
    b_, nc, _, _ = aff.shape
    blk = pl.BlockSpec((1, nc, N_EXPERTS, LANES), lambda b: (b, 0, 0, 0))
    return pl.pallas_call(
        functools.partial(_route_kernel, cap=cap),
        grid=(b_,),
        in_specs=[blk, pl.BlockSpec((LANES, LANES), lambda b: (0, 0))],
        out_specs=[blk, blk, blk],
        out_shape=[jax.ShapeDtypeStruct(aff.shape, I32), jax.ShapeDtypeStruct(aff.shape, F32),
                   jax.ShapeDtypeStruct(aff.shape, I32)],
        compiler_params=_cparams(("parallel",)),
        name="route",
    )(aff, tri)


def _slab_geometry(starts_ref, b, j, e):
    start = starts_ref[b, j, e]
    count = starts_ref[b, j + 1, e] - start
    base = (start // BF16_ROWS) * BF16_ROWS
    nslab = (start - base + count + SLAB - 1) // SLAB
    return base, jnp.where(count > 0, nslab, 0)


def _dispatch_kernel(starts_ref, x_ref, pos_ref, xe_ref):
    b = pl.program_id(0)
    eg = pl.program_id(1)
    j = pl.program_id(2)
    t = x_ref.shape[1]

    @pl.when(j == 0)
    def _zero():
        xe_ref[...] = jnp.zeros(xe_ref.shape, BF16)

    xt = x_ref[0]
    rid = lax.broadcasted_iota(I32, (SLAB, t), 0)
    geo, local = [], []
    for el in range(E_GROUP):
        e = eg * E_GROUP + el
        geo.append(_slab_geometry(starts_ref, b, j, e))
        prow = jnp.concatenate(
            [pos_ref[0, c, pl.ds(e, 1), :] for c in range(t // LANES)], axis=1)
        local.append(jnp.broadcast_to(prow - geo[el][0], (SLAB, t)))

    def add_rows(el, k, rows):
        dst = pl.ds(pl.multiple_of(geo[el][0] + k * SLAB, BF16_ROWS), SLAB)
        xe_ref[0, el, dst, :] = xe_ref[0, el, dst, :] + rows.astype(BF16)

    onehot = jnp.concatenate([(loc == rid).astype(BF16) for loc in local], axis=0)
    rows = _dot(onehot, xt)
    for el in range(E_GROUP):
        add_rows(el, 0, rows[el * SLAB:(el + 1) * SLAB])

    for el in range(E_GROUP):
        def body(k, carry, el=el):
            add_rows(el, k, _dot((local[el] == rid + k * SLAB).astype(BF16), xt))
            return carry

        lax.fori_loop(1, geo[el][1], body, 0)


def _dispatch(starts, x1b, pos, capp):
    b_, s_, d_ = x1b.shape
    t = min(T_TOK, s_)
    grid_spec = pltpu.PrefetchScalarGridSpec(
        num_scalar_prefetch=1,
        grid=(b_, N_EXPERTS // E_GROUP, s_ // t),
        in_specs=[
            pl.BlockSpec((1, t, d_), lambda b, g, j, *_: (b, j, 0)),
            pl.BlockSpec((1, t // LANES, N_EXPERTS, LANES), lambda b, g, j, *_: (b, j, 0, 0)),
        ],
        out_specs=pl.BlockSpec((1, E_GROUP, capp, d_), lambda b, g, j, *_: (b, g, 0, 0)),
    )
    return pl.pallas_call(
        _dispatch_kernel,
        grid_spec=grid_spec,
        out_shape=jax.ShapeDtypeStruct((b_, N_EXPERTS, capp, d_), BF16),
        compiler_params=_cparams(("parallel", "parallel", "arbitrary")),
        name="dispatch",
    )(starts, x1b, pos)


def _ffn_kernel(xe_ref, wg_ref, wu_ref, wd_ref, y_ref, acc_ref, *, cap):
    e = pl.program_id(0)
    fc = pl.program_id(1)
    b = pl.program_id(2)

    @pl.when((e == 0) & (fc == 0) & (b == 0))
    def _define_acc():
        acc_ref[...] = jnp.zeros(acc_ref.shape, F32)

    xe = xe_ref[0, 0]
    hg = _dot(xe, wg_ref[0].astype(BF16))
    hu = _dot(xe, wu_ref[0].astype(BF16))
    act = (hg * jax.nn.sigmoid(hg) * hu).astype(BF16)
    total = jnp.where(fc > 0, acc_ref[b], 0.0) + _dot(act, wd_ref[0].astype(BF16))
    acc_ref[b] = total
    y_ref[0, 0, 0:cap, :] = total.astype(BF16)
    y_ref[0, 0, cap:, :] = jnp.zeros((y_ref.shape[2] - cap, y_ref.shape[3]), BF16)


def _ffn(xe, w_gate, w_up, w_down, cap):
    b_, ne, capp, d_ = xe.shape
    f_ = w_gate.shape[2]
    fcw = min(FC, f_)
    nf = f_ // fcw
    return pl.pallas_call(
        functools.partial(_ffn_kernel, cap=cap),
        grid=(ne, nf, b_),
        in_specs=[
            pl.BlockSpec((1, 1, cap, d_), lambda e, f, b: (b, e, 0, 0)),
            pl.BlockSpec((1, d_, fcw), lambda e, f, b: (e, 0, f)),
            pl.BlockSpec((1, d_, fcw), lambda e, f, b: (e, 0, f)),
            pl.BlockSpec((1, fcw, d_), lambda e, f, b: (e, f, 0)),
        ],
        out_specs=pl.BlockSpec((1, 1, capp, d_), lambda e, f, b: (jnp.where(f == nf - 1, b, 0), e, 0, 0)),
        out_shape=jax.ShapeDtypeStruct((b_, ne, capp, d_), BF16),
        scratch_shapes=[pltpu.VMEM((b_, cap, d_), F32)],
        compiler_params=_cparams(("arbitrary", "arbitrary", "arbitrary")),
        name="ffn",
    )(xe, w_gate, w_up, w_down)


def _combine_kernel(starts_ref, y_hbm, pos_ref, gate_ref, x1_ref, p_ref, ex_ref, wpg_ref, wpp_ref,
                    g2_ref, b2_ref, g3_ref, b3_ref, o_ref, ybuf, xbuf, sem, xsem, moe_ref, *, alpha):
    b = pl.program_id(0)
    j = pl.program_id(1)
    nt = pl.num_programs(1)
    t = x1_ref.shape[1]
    slot = j % 2

    def first_slab_copy(jj, e, sl):
        base, _ = _slab_geometry(starts_ref, b, jj, e)
        src = y_hbm.at[b, e, pl.ds(pl.multiple_of(base, BF16_ROWS), SLAB), :]
        return pltpu.make_async_copy(src, ybuf.at[sl, pl.ds(e * SLAB, SLAB), :], sem.at[sl, e])

    @pl.when(j == 0)
    def _prime():
        for e in range(N_EXPERTS):
            first_slab_copy(j, e, slot).start()

    @pl.when(j + 1 < nt)
    def _prefetch_next_tile():
        for e in range(N_EXPERTS):
            first_slab_copy(j + 1, e, 1 - slot).start()

    pad = jnp.zeros((LANES - N_EXPERTS, LANES), F32)
    pos_t = jnp.concatenate(
        [jnp.concatenate([pos_ref[0, c].astype(F32), pad], axis=0).T for c in range(t // LANES)], axis=0)
    gate_t = jnp.concatenate(
        [jnp.concatenate([gate_ref[0, c], pad], axis=0).T for c in range(t // LANES)], axis=0)

    geo = [_slab_geometry(starts_ref, b, j, e) for e in range(N_EXPERTS)]
    lane_e = lax.broadcasted_iota(I32, (1, LANES), 1)
    slab_e = lax.broadcasted_iota(I32, (1, N_EXPERTS * SLAB), 1) // SLAB
    start_row = jnp.zeros((1, LANES), F32)
    shift_row = jnp.zeros((1, N_EXPERTS * SLAB), F32)
    for e, (base, _) in enumerate(geo):
        start = starts_ref[b, j, e]
        start_row = jnp.where(lane_e == e, start.astype(F32), start_row)
        shift_row = jnp.where(slab_e == e, (start - base).astype(F32), shift_row)
    rank = jnp.where(pos_t >= 0.0, pos_t - start_row, -512.0).astype(BF16)
    rank_x = _dot(rank, ex_ref[...])
    row_in_slab = (lax.broadcasted_iota(I32, (1, N_EXPERTS * SLAB), 1) % SLAB).astype(F32)
    match = rank_x == row_in_slab - shift_row
    g_hi, g_lo = _split_bf16(gate_t)
    oh_hi = jnp.where(match, _dot(g_hi, ex_ref[...]), 0.0).astype(BF16)
    oh_lo = jnp.where(match, _dot(g_lo, ex_ref[...]), 0.0).astype(BF16)

    moe_ref[...] = jnp.zeros(moe_ref.shape, F32)
    cid = lax.broadcasted_iota(I32, (t, SLAB), 1).astype(F32)
    for e, (base, nslab) in enumerate(geo):
        def body(k, carry, e=e, base=base):
            src = y_hbm.at[b, e, pl.ds(pl.multiple_of(base + k * SLAB, BF16_ROWS), SLAB), :]
            cp = pltpu.make_async_copy(src, xbuf, xsem)
            cp.start()
            local = jnp.broadcast_to(pos_t[:, e:e + 1] - (base + k * SLAB).astype(F32), (t, SLAB))
            onehot = (local == cid).astype(BF16)
            cp.wait()
            moe_ref[...] = moe_ref[...] + gate_t[:, e:e + 1] * _dot(onehot, xbuf[...])
            return carry

        lax.fori_loop(1, nslab, body, 0)

    for e in range(N_EXPERTS):
        first_slab_copy(j, e, slot).wait()
    ycat = ybuf[slot]
    moe = moe_ref[...] + _dot(oh_hi, ycat) + _dot(oh_lo, ycat)
    x2 = _layer_norm(alpha * x1_ref[0] + moe, g2_ref[...], b2_ref[...])
    gate = jax.nn.sigmoid(_dot(x2.astype(BF16), wpg_ref[...]))
    ple = _dot(p_ref[0].astype(BF16), wpp_ref[...]) * gate
    o_ref[0] = _layer_norm(alpha * x2 + ple, g3_ref[...], b3_ref[...])


def _combine(starts, y, pos, gate, x1, p, wpg, wpp, g2, b2, g3, b3, alpha):
    b_, s_, d_ = x1.shape
    t = min(T_TOK, s_)
    pd = p.shape[2]
    ex = (jnp.arange(LANES)[:, None] == jnp.arange(N_EXPERTS * SLAB)[None, :] // SLAB).astype(BF16)
    full = lambda shape: pl.BlockSpec(shape, lambda b, j, *_: (0,) * len(shape))
    tile4 = pl.BlockSpec((1, t // LANES, N_EXPERTS, LANES), lambda b, j, *_: (b, j, 0, 0))
    grid_spec = pltpu.PrefetchScalarGridSpec(
        num_scalar_prefetch=1,
        grid=(b_, s_ // t),
        in_specs=[
            pl.BlockSpec(memory_space=pl.ANY),
            tile4, tile4,
            pl.BlockSpec((1, t, d_), lambda b, j, *_: (b, j, 0)),
            pl.BlockSpec((1, t, pd), lambda b, j, *_: (b, j, 0)),
            full((LANES, N_EXPERTS * SLAB)),
            full((d_, d_)), full((pd, d_)),
            full((1, d_)), full((1, d_)), full((1, d_)), full((1, d_)),
        ],
        out_specs=pl.BlockSpec((1, t, d_), lambda b, j, *_: (b, j, 0)),
        scratch_shapes=[pltpu.VMEM((2, N_EXPERTS * SLAB, d_), BF16),
                        pltpu.VMEM((SLAB, d_), BF16),
                        pltpu.SemaphoreType.DMA((2, N_EXPERTS)),
                        pltpu.SemaphoreType.DMA(()),
                        pltpu.VMEM((t, d_), F32)],
    )
    return pl.pallas_call(
        functools.partial(_combine_kernel, alpha=alpha),
        grid_spec=grid_spec,
        out_shape=jax.ShapeDtypeStruct((b_, s_, d_), F32),
        compiler_params=_cparams(("parallel", "arbitrary")),
        name="combine",
    )(starts, y, pos, gate, x1, p, ex, wpg, wpp, g2, b2, g3, b3)


def _rope_tables(s_):
    half = HEAD_DIM // 2
    inv = ROPE_THETA ** (-jnp.arange(0, half, 2, dtype=F32) / half)
    t = jnp.arange(s_, dtype=jnp.int32)
    row = (t // GRID_W).astype(F32)[:, None] * inv[None, :]
    col = (t % GRID_W).astype(F32)[:, None] * inv[None, :]
    cos = jnp.concatenate([jnp.cos(row), jnp.cos(row), jnp.cos(col), jnp.cos(col)], axis=1)
    sin = jnp.concatenate([-jnp.sin(row), jnp.sin(row), -jnp.sin(col), jnp.sin(col)], axis=1)
    reps = LANES // HEAD_DIM
    return jnp.tile(cos, (1, reps)), jnp.tile(sin, (1, reps))


def _t5_bucket(rel):
    half = NUM_BUCKETS // 2
    max_exact = half // 2
    ret = (rel > 0).astype(jnp.int32) * half
    n = jnp.abs(rel)
    nf = jnp.maximum(n, 1).astype(F32)
    large = max_exact + (jnp.log(nf / max_exact) / math.log(MAX_DISTANCE / max_exact)
                         * (half - max_exact)).astype(jnp.int32)
    large = jnp.minimum(large, half - 1)
    return ret + jnp.where(n < max_exact, n, large)


def _bucket_thresholds():
    half = NUM_BUCKETS // 2
    n = jnp.arange(0, MAX_DISTANCE + 1, dtype=jnp.int32)
    bk = _t5_bucket(-n)
    j = jnp.arange(half, dtype=jnp.int32)
    return jnp.sum((bk[None, :] < j[:, None]).astype(jnp.int32), axis=1)


def _group_mean_matrix(width):
    g = jnp.arange(width, dtype=jnp.int32) // HEAD_DIM
    return ((g[:, None] == g[None, :]).astype(F32) / HEAD_DIM).astype(BF16)


def kernel(x, p, w_in, w_out, a_q_norm, a_k_norm, b_lambda_q1, b_lambda_k1, b_lambda_q2, b_lambda_k2,
           b_subln, rel_bias, ln1_g, ln1_b, w_router, w_gate, w_up, w_down, ln2_g, ln2_b,
           w_ple_gate, w_ple_proj, ln3_g, ln3_b):
    b_, s_, d_ = x.shape
    depth = w_in.shape[0]
    alpha = (2 * depth) ** 0.25
    cap = EC_CAPACITY_FACTOR * s_ // N_EXPERTS
    t_tok = min(T_TOK, s_)
    capp = cap + SLAB
    assert s_ % GRID_W == 0 and s_ % LANES == 0 and cap % BF16_ROWS == 0

    cs, sn = _rope_tables(s_)
    thr = _bucket_thresholds()
    bias_vals = (rel_bias.astype(F32) * LOG2E).T
    gmq, gmk = _group_mean_matrix(A_Q), _group_mean_matrix(A_KV)
    tri = (jnp.arange(LANES)[:, None] <= jnp.arange(LANES)[None, :]).astype(BF16)
    row = lambda v: v.astype(F32).reshape(1, -1)

    for i in range(depth):
        lam_init = 0.8 - 0.6 * math.exp(-0.3 * i)
        qa, ka, vat, qb, kb, vbt, qb_sq, kb_sq = _inproj(
            x, w_in[i].astype(BF16), cs, sn,
            jnp.tile(row(a_q_norm[i]), (1, A_HEADS)), jnp.tile(row(a_k_norm[i]), (1, A_KV_HEADS)), gmq, gmk)
        bound_a = (HEAD_DIM * ATTN_SCALE * LOG2E * 1.01
                   * jnp.max(jnp.abs(a_q_norm[i].astype(F32))) * jnp.max(jnp.abs(a_k_norm[i].astype(F32))))
        bound_b = 1.03 * jnp.sqrt(jnp.max(qb_sq) * jnp.max(kb_sq)) + jnp.max(jnp.abs(bias_vals))
        score_bytes = 2 * s_ * 4 * max(min(TQ_A_BOUNDED, s_) // A_QUERY_SPLIT, min(T_B, s_) // B_QUERY_SPLIT)
        fits = score_bytes <= VMEM_LIMIT // 2
        oa = lax.cond(fits & (bound_a <= EXP_RANGE),
                      lambda *a: _attn_a(*a, bounded=True), lambda *a: _attn_a(*a, bounded=False), qa, ka, vat)
        b_args = (qb, kb, vbt, thr, bias_vals, row(b_lambda_q1[i]), row(b_lambda_k1[i]),
                  row(b_lambda_q2[i]), row(b_lambda_k2[i]), b_subln[i].astype(F32).reshape(-1, 1))
        ob = lax.cond(fits & (bound_b <= EXP_RANGE),
                      lambda *a: _attn_b(*a, lam_init, bounded=True),
                      lambda *a: _attn_b(*a, lam_init, bounded=False), *b_args)
        wr = jnp.pad(w_router[i].astype(F32), ((0, 0), (0, LANES - N_EXPERTS)))
        rh = wr.astype(BF16)
        rl = (wr - rh.astype(F32)).astype(BF16)
        wo = w_out[i].astype(BF16)
        x1, x1b, aff = _outproj(oa, ob, x, wo[:A_Q], wo[A_Q:], row(ln1_g[i]), row(ln1_b[i]),
                                jnp.concatenate([rh, rl], axis=1), alpha)
        pos, gate, off = _route(aff, tri, cap)
        starts = jnp.concatenate(
            [off[:, ::t_tok // LANES, :, 0], jnp.full((b_, 1, N_EXPERTS), cap, jnp.int32)], axis=1)
        xe = _dispatch(starts, x1b, pos, capp)
        y = _ffn(xe, w_gate[i], w_up[i], w_down[i], cap)
        x = _combine(starts, y, pos, gate, x1, p[i], w_ple_gate[i].astype(BF16), w_ple_proj[i].astype(BF16),
                     row(ln2_g[i]), row(ln2_b[i]), row(ln3_g[i]), row(ln3_b[i]), alpha)
    return x
```
